```python
import math, functools
import jax, jax.numpy as jnp
from jax import lax
import numpy as np

D_MODEL = 1024
BATCH = 8
SEQ = 2048
DEPTH = 1
DEC_BATCH = 128
DEC_SEQ = 8
PAST_LEN = 8192
PAGE_SIZE = 128

ATTN_WIDTH = D_MODEL // 2
HEAD_DIM = 64
N_HEADS = ATTN_WIDTH // (2 * HEAD_DIM)
ROT_DIM = HEAD_DIM // 4
ROPE_THETA = 500000.0
BLOCK_Q = 128
SSM_WIDTH = D_MODEL - ATTN_WIDTH
SSM_GROUP = 16
SSM_GROUPS = SSM_WIDTH // SSM_GROUP
SSM_STATE = 64
DT_MIN = 0.001
DT_MAX = 0.1
MIX_IN = 3 * ATTN_WIDTH + SSM_WIDTH
MIX_OUT = ATTN_WIDTH + SSM_WIDTH
N_MEM = 256
CROSS_HEADS = 4
CROSS_HEAD_DIM = 64
CROSS_WIDTH = CROSS_HEADS * CROSS_HEAD_DIM
D_FF = 4 * D_MODEL
EPS = 1e-6
NEG_INF = -1e30

kernel_name = 'hymba_diffattn_s5_memxattn_decode_step'


def rms_norm(x, g):
    xf = x.astype(jnp.float32)
    y = xf * lax.rsqrt(jnp.mean(xf * xf, axis=-1, keepdims=True) + EPS)
    return (y * g.astype(jnp.float32)).astype(x.dtype)


def lambda_init_for(layer):
    return 0.8 - 0.6 * math.exp(-0.3 * layer)


def rope_partial(x, pos):
    half = ROT_DIM // 2
    inv_freq = jnp.float32(ROPE_THETA) ** (-jnp.arange(half, dtype=jnp.float32) * 2.0 / ROT_DIM)
    ang = pos.astype(jnp.float32)[:, None] * inv_freq[None, :]
    cos = jnp.cos(ang)[None, :, None, None, :]
    sin = jnp.sin(ang)[None, :, None, None, :]
    xf = x.astype(jnp.float32)
    x1 = xf[..., :half]
    x2 = xf[..., half:ROT_DIM]
    out = jnp.concatenate([x1 * cos - x2 * sin, x2 * cos + x1 * sin, xf[..., ROT_DIM:]], axis=-1)
    return out.astype(x.dtype)


def diff_attend_block(q, k, v, mask, lam):
    s = jnp.einsum('nqhjd,nkhjd->nhjqk', q.astype(jnp.float32), k.astype(jnp.float32)) * (1.0 / math.sqrt(HEAD_DIM))
    s = jnp.where(mask, s, NEG_INF)
    p = jax.nn.softmax(s, axis=-1)
    pd = p[:, :, 0] - lam * p[:, :, 1]
    return jnp.einsum('nhqk,nkhe->nqhe', pd, v.astype(jnp.float32))


def attend_prompt(q, k, v, lam):
    n, t = q.shape[0], q.shape[1]
    nb = t // BLOCK_Q
    qb = jnp.moveaxis(q.reshape(n, nb, BLOCK_Q, N_HEADS, 2, HEAD_DIM), 1, 0)
    kpos = jnp.arange(t)

    def block(args):
        qi, bi = args
        qpos = bi * BLOCK_Q + jnp.arange(BLOCK_Q)
        mask = kpos[None, :] <= qpos[:, None]
        return diff_attend_block(qi, k, v, mask, lam)

    out = lax.map(block, (qb, jnp.arange(nb)))
    return jnp.moveaxis(out, 0, 1).reshape(n, t, N_HEADS, 2 * HEAD_DIM)


def attend_paged(q, k, v, lam, cache_k, cache_v, page_table, layer):
    t = q.shape[1]
    past = page_table.shape[1] * PAGE_SIZE
    mask = jnp.concatenate([jnp.ones((t, past), dtype=bool), jnp.tril(jnp.ones((t, t), dtype=bool))], axis=1)

    def one(args):
        qi, ki, vi, row = args
        kp = cache_k[layer, row].reshape(past, N_HEADS, 2, HEAD_DIM)
        vp = cache_v[layer, row].reshape(past, N_HEADS, 2 * HEAD_DIM)
        kk = jnp.concatenate([kp, ki.astype(kp.dtype)], axis=0)
        vv = jnp.concatenate([vp, vi.astype(vp.dtype)], axis=0)
        return diff_attend_block(qi[None], kk[None], vv[None], mask, lam)[0]

    return lax.map(one, (q, k, v, page_table))


def complex_affine_combine(e1, e2):
    a1r, a1i, b1r, b1i = e1
    a2r, a2i, b2r, b2i = e2
    return (a2r * a1r - a2i * a1i,
            a2r * a1i + a2i * a1r,
            a2r * b1r - a2i * b1i + b2r,
            a2r * b1i + a2i * b1r + b2i)


def s5_group(u, h_re, h_im, a_re, a_im, log_dt, b_re, b_im, c_re, c_im, d_skip, w_glu, b_glu):
    n, t = u.shape[0], u.shape[1]
    uf = u.astype(jnp.float32)
    delta = jnp.exp(log_dt.astype(jnp.float32))[:, None]
    ar = a_re.astype(jnp.float32)
    ai = a_im.astype(jnp.float32)
    mag = jnp.exp(ar * delta)
    ang = ai * delta
    lb_re = mag * jnp.cos(ang)
    lb_im = mag * jnp.sin(ang)
    den = ar * ar + ai * ai
    num_re = lb_re - 1.0
    cz_re = (num_re * ar + lb_im * ai) / den
    cz_im = (lb_im * ar - num_re * ai) / den
    br = b_re.astype(jnp.float32)
    bi = b_im.astype(jnp.float32)
    bb_re = cz_re[..., None] * br - cz_im[..., None] * bi
    bb_im = cz_re[..., None] * bi + cz_im[..., None] * br
    bu_re = jnp.einsum('ntgp,gsp->ntgs', uf, bb_re)
    bu_im = jnp.einsum('ntgp,gsp->ntgs', uf, bb_im)
    h0r = h_re.astype(jnp.float32)
    h0i = h_im.astype(jnp.float32)
    bu_re = bu_re.at[:, 0].add(lb_re * h0r - lb_im * h0i)
    bu_im = bu_im.at[:, 0].add(lb_re * h0i + lb_im * h0r)
    shape = bu_re.shape
    elems = (jnp.broadcast_to(lb_re, shape), jnp.broadcast_to(lb_im, shape), bu_re, bu_im)
    _, _, xr, xi = lax.associative_scan(complex_affine_combine, elems, axis=1)
    y = (jnp.einsum('ntgs,gps->ntgp', xr, c_re.astype(jnp.float32))
         - jnp.einsum('ntgs,gps->ntgp', xi, c_im.astype(jnp.float32))
         + d_skip.astype(jnp.float32) * uf)
    y = y.reshape(n, t, SSM_WIDTH)
    g = jax.nn.gelu(y, approximate=False)
    out = g * jax.nn.sigmoid(g @ w_glu.astype(jnp.float32) + b_glu.astype(jnp.float32))
    return out, xr[:, -1], xi[:, -1]


def memory_kv(mem, norm_mem, w_ck, w_cv):
    n, m = mem.shape[0], mem.shape[1]
    mn = rms_norm(mem, norm_mem)
    mk = (mn @ w_ck).reshape(n, m, CROSS_HEADS, CROSS_HEAD_DIM)
    mv = (mn @ w_cv).reshape(n, m, CROSS_HEADS, CROSS_HEAD_DIM)
    return mk, mv


def hybrid_layer(x, pos, attend, h_re, h_im, mem_k, mem_v, lam_init, p):
    n, t = x.shape[0], x.shape[1]
    dt = x.dtype
    h = rms_norm(x, p['norm_mix'])
    proj = h @ p['w_in']
    q = proj[..., :ATTN_WIDTH].reshape(n, t, N_HEADS, 2, HEAD_DIM)
    k = proj[..., ATTN_WIDTH:2 * ATTN_WIDTH].reshape(n, t, N_HEADS, 2, HEAD_DIM)
    v = proj[..., 2 * ATTN_WIDTH:3 * ATTN_WIDTH].reshape(n, t, N_HEADS, 2 * HEAD_DIM)
    u = proj[..., 3 * ATTN_WIDTH:].reshape(n, t, SSM_GROUPS, SSM_GROUP)
    q = rope_partial(q, pos)
    k = rope_partial(k, pos)
    lam = (jnp.exp(jnp.sum(p['lambda_q1'].astype(jnp.float32) * p['lambda_k1'].astype(jnp.float32)))
           - jnp.exp(jnp.sum(p['lambda_q2'].astype(jnp.float32) * p['lambda_k2'].astype(jnp.float32)))
           + lam_init)
    o = attend(q, k, v, lam)
    o = (rms_norm(o, p['subln_gain']) * (1.0 - lam_init)).reshape(n, t, ATTN_WIDTH)
    ys, h_re_new, h_im_new = s5_group(u, h_re, h_im, p['ssm_a_re'], p['ssm_a_im'], p['ssm_log_dt'],
                                      p['ssm_b_re'], p['ssm_b_im'], p['ssm_c_re'], p['ssm_c_im'],
                                      p['ssm_d'], p['w_glu'], p['b_glu'])
    mix = jnp.concatenate([o, ys], axis=-1).astype(dt) @ p['w_out']
    x = x + mix
    hq = (rms_norm(x, p['norm_cross']) @ p['w_cq']).reshape(n, t, CROSS_HEADS, CROSS_HEAD_DIM)
    s = jnp.einsum('nqhd,nkhd->nhqk', hq.astype(jnp.float32), mem_k.astype(jnp.float32)) * (1.0 / math.sqrt(CROSS_HEAD_DIM))
    pr = jax.nn.softmax(s, axis=-1)
    oc = jnp.einsum('nhqk,nkhd->nqhd', pr, mem_v.astype(jnp.float32)).reshape(n, t, CROSS_WIDTH)
    x = x + oc.astype(dt) @ p['w_co']
    z = rms_norm(x, p['norm_mlp']) @ p['w_up']
    x = x + jnp.square(jax.nn.relu(z)) @ p['w_down']
    return x, k, v, h_re_new, h_im_new


def setup_inputs(seed: int = 0) -> dict:
    key = jax.random.key(seed)
    keys = jax.random.split(key, 48)
    ki = iter(range(48))
    n_pages = PAST_LEN // PAGE_SIZE
    n_pool = (DEC_BATCH * n_pages * 5) // 4
    f32 = jnp.float32

    def nrm(shape, scale):
        return scale * jax.random.normal(keys[next(ki)], shape, f32)

    def gain(shape):
        return 1.0 + nrm(shape, 0.02)

    x_prompt = nrm((BATCH, SEQ, D_MODEL), 1.0)
    x_sample = nrm((DEC_BATCH, DEC_SEQ, D_MODEL), 1.0)
    mem_prompt = nrm((BATCH, N_MEM, D_MODEL), 1.0)
    cache_k = nrm((DEPTH, n_pool, PAGE_SIZE, N_HEADS, 2, HEAD_DIM), 1.0)
    cache_v = nrm((DEPTH, n_pool, PAGE_SIZE, N_HEADS, 2 * HEAD_DIM), 1.0)
    page_table = jax.random.permutation(keys[next(ki)], n_pool)[:DEC_BATCH * n_pages].reshape(DEC_BATCH, n_pages).astype(jnp.int32)
    state_ssm_re = nrm((DEPTH, DEC_BATCH, SSM_GROUPS, SSM_STATE), 0.1)
    state_ssm_im = nrm((DEPTH, DEC_BATCH, SSM_GROUPS, SSM_STATE), 0.1)
    cache_mem_k = nrm((DEPTH, DEC_BATCH, N_MEM, CROSS_HEADS, CROSS_HEAD_DIM), 1.0)
    cache_mem_v = nrm((DEPTH, DEC_BATCH, N_MEM, CROSS_HEADS, CROSS_HEAD_DIM), 1.0)
    ssm_a_re = -0.5 + nrm((DEPTH, SSM_GROUPS, SSM_STATE), 0.01)
    ssm_a_im = math.pi * jnp.arange(SSM_STATE, dtype=f32) + nrm((DEPTH, SSM_GROUPS, SSM_STATE), 0.01)
    ssm_log_dt = jax.random.uniform(keys[next(ki)], (DEPTH, SSM_GROUPS), f32, math.log(DT_MIN), math.log(DT_MAX))
    return {
        'x_prompt': x_prompt,
        'x_sample': x_sample,
        'mem_prompt': mem_prompt,
        'cache_k': cache_k,
        'cache_v': cache_v,
        'page_table': page_table,
        'state_ssm_re': state_ssm_re,
        'state_ssm_im': state_ssm_im,
        'cache_mem_k': cache_mem_k,
        'cache_mem_v': cache_mem_v,
        'norm_mix': gain((DEPTH, D_MODEL)),
        'w_in': nrm((DEPTH, D_MODEL, MIX_IN), D_MODEL ** -0.5),
        'lambda_q1': nrm((DEPTH, HEAD_DIM), 0.1),
        'lambda_k1': nrm((DEPTH, HEAD_DIM), 0.1),
        'lambda_q2': nrm((DEPTH, HEAD_DIM), 0.1),
        'lambda_k2': nrm((DEPTH, HEAD_DIM), 0.1),
        'subln_gain': gain((DEPTH, 2 * HEAD_DIM)),
        'ssm_a_re': ssm_a_re,
        'ssm_a_im': ssm_a_im,
        'ssm_log_dt': ssm_log_dt,
        'ssm_b_re': nrm((DEPTH, SSM_GROUPS, SSM_STATE, SSM_GROUP), (2.0 * SSM_GROUP) ** -0.5),
        'ssm_b_im': nrm((DEPTH, SSM_GROUPS, SSM_STATE, SSM_GROUP), (2.0 * SSM_GROUP) ** -0.5),
        'ssm_c_re': nrm((DEPTH, SSM_GROUPS, SSM_GROUP, SSM_STATE), (2.0 * SSM_STATE) ** -0.5),
        'ssm_c_im': nrm((DEPTH, SSM_GROUPS, SSM_GROUP, SSM_STATE), (2.0 * SSM_STATE) ** -0.5),
        'ssm_d': nrm((DEPTH, SSM_GROUPS, SSM_GROUP), 1.0),
        'w_glu': nrm((DEPTH, SSM_WIDTH, SSM_WIDTH), SSM_WIDTH ** -0.5),
        'b_glu': nrm((DEPTH, SSM_WIDTH), 0.01),
        'w_out': nrm((DEPTH, MIX_OUT, D_MODEL), MIX_OUT ** -0.5),
        'norm_cross': gain((DEPTH, D_MODEL)),
        'norm_mem': gain((DEPTH, D_MODEL)),
        'w_cq': nrm((DEPTH, D_MODEL, CROSS_WIDTH), D_MODEL ** -0.5),
        'w_ck': nrm((DEPTH, D_MODEL, CROSS_WIDTH), D_MODEL ** -0.5),
        'w_cv': nrm((DEPTH, D_MODEL, CROSS_WIDTH), D_MODEL ** -0.5),
        'w_co': nrm((DEPTH, CROSS_WIDTH, D_MODEL), CROSS_WIDTH ** -0.5),
        'norm_mlp': gain((DEPTH, D_MODEL)),
        'w_up': nrm((DEPTH, D_MODEL, D_FF), D_MODEL ** -0.5),
        'w_down': nrm((DEPTH, D_FF, D_MODEL), D_FF ** -0.5),
        'final_norm': gain((D_MODEL,)),
    }


def reference(x_prompt, x_sample, mem_prompt, cache_k, cache_v, page_table, state_ssm_re, state_ssm_im,
              cache_mem_k, cache_mem_v, norm_mix, w_in, lambda_q1, lambda_k1, lambda_q2, lambda_k2,
              subln_gain, ssm_a_re, ssm_a_im, ssm_log_dt, ssm_b_re, ssm_b_im, ssm_c_re, ssm_c_im, ssm_d,
              w_glu, b_glu, w_out, norm_cross, norm_mem, w_cq, w_ck, w_cv, w_co, norm_mlp, w_up, w_down,
              final_norm):
    n_p, t_p = x_prompt.shape[0], x_prompt.shape[1]
    t_s = x_sample.shape[1]
    past = page_table.shape[1] * PAGE_SIZE
    pos_prompt = jnp.arange(t_p, dtype=jnp.int32)
    pos_sample = past + jnp.arange(t_s, dtype=jnp.int32)
    h0 = jnp.zeros((n_p, SSM_GROUPS, SSM_STATE), jnp.float32)
    yp = x_prompt
    ys = x_sample
    kp_l, vp_l, srp_l, sip_l, mkp_l, mvp_l = [], [], [], [], [], []
    ks_l, vs_l, srs_l, sis_l = [], [], [], []
    for l in range(DEPTH):
        lam_init = lambda_init_for(l)
        p = {
            'norm_mix': norm_mix[l], 'w_in': w_in[l],
            'lambda_q1': lambda_q1[l], 'lambda_k1': lambda_k1[l],
            'lambda_q2': lambda_q2[l], 'lambda_k2': lambda_k2[l],
            'subln_gain': subln_gain[l],
            'ssm_a_re': ssm_a_re[l], 'ssm_a_im': ssm_a_im[l], 'ssm_log_dt': ssm_log_dt[l],
            'ssm_b_re': ssm_b_re[l], 'ssm_b_im': ssm_b_im[l],
            'ssm_c_re': ssm_c_re[l], 'ssm_c_im': ssm_c_im[l], 'ssm_d': ssm_d[l],
            'w_glu': w_glu[l], 'b_glu': b_glu[l], 'w_out': w_out[l],
            'norm_cross': norm_cross[l], 'w_cq': w_cq[l], 'w_co': w_co[l],
            'norm_mlp': norm_mlp[l], 'w_up': w_up[l], 'w_down': w_down[l],
        }
        mk_p, mv_p = memory_kv(mem_prompt, norm_mem[l], w_ck[l], w_cv[l])
        yp, kp, vp, srp, sip = hybrid_layer(yp, pos_prompt, attend_prompt, h0, h0, mk_p, mv_p, lam_init, p)
        attend_s = functools.partial(attend_paged, cache_k=cache_k, cache_v=cache_v, page_table=page_table, layer=l)
        ys, ks, vs, srs, sis = hybrid_layer(ys, pos_sample, attend_s, state_ssm_re[l], state_ssm_im[l],
                                            cache_mem_k[l], cache_mem_v[l], lam_init, p)
        kp_l.append(kp); vp_l.append(vp); srp_l.append(srp); sip_l.append(sip)
        mkp_l.append(mk_p); mvp_l.append(mv_p)
        ks_l.append(ks); vs_l.append(vs); srs_l.append(srs); sis_l.append(sis)
    y_prompt = rms_norm(yp, final_norm)
    y_sample = rms_norm(ys, final_norm)
    k_prompt = jnp.stack(kp_l, axis=0)
    v_prompt = jnp.stack(vp_l, axis=0)
    ssm_re_prompt = jnp.stack(srp_l, axis=0)
    ssm_im_prompt = jnp.stack(sip_l, axis=0)
    mem_k_prompt = jnp.stack(mkp_l, axis=0)
    mem_v_prompt = jnp.stack(mvp_l, axis=0)
    k_sample = jnp.stack(ks_l, axis=0)
    v_sample = jnp.stack(vs_l, axis=0)
    ssm_re_sample = jnp.stack(srs_l, axis=0)
    ssm_im_sample = jnp.stack(sis_l, axis=0)
    return (y_prompt, y_sample, k_prompt, v_prompt, ssm_re_prompt, ssm_im_prompt, mem_k_prompt, mem_v_prompt,
            k_sample, v_sample, ssm_re_sample, ssm_im_sample)
```

```python
import functools
import math

import jax
import jax.numpy as jnp
from jax import lax
from jax.experimental import pallas as pl
from jax.experimental.pallas import tpu as pltpu

F32 = jnp.float32
BF16 = jnp.bfloat16

D_MODEL = 1024
HEAD_DIM = 64
N_HEADS = 4
ATTN_WIDTH = 512
ROT_DIM = 16
ROPE_THETA = 500000.0
SSM_WIDTH = 512
SSM_GROUP = 16
SSM_GROUPS = 32
SSM_STATE = 64
N_STATE = SSM_GROUPS * SSM_STATE
PAGE_SIZE = 128
N_MEM = 256
CROSS_WIDTH = 256
CROSS_HEAD_DIM = 64
D_FF = 4096
EPS = 1e-6
NEG_INF = -1e30
LAM_INIT = 0.8 - 0.6 * math.exp(-0.3 * 0)

LANES = 128
SUBLANES = 8
VMEM_LIMIT = 56 * 1024 * 1024

TOK_TILE = 256
PAGES_PER_CHUNK = 16


def _params(sem):
    return pltpu.CompilerParams(dimension_semantics=sem, vmem_limit_bytes=VMEM_LIMIT)


def _rms(x, g):
    ms = jnp.mean(x * x, axis=-1, keepdims=True)
    return x * lax.rsqrt(ms + EPS) * g


def _const_spec(shape):
    nd = len(shape)
    return pl.BlockSpec(shape, lambda *_: (0,) * nd)


def _rope(x, c, s1, s2):
    outs = []
    for i in range(x.shape[1] // LANES):
        xc = x[:, i * LANES:(i + 1) * LANES]
        outs.append(xc * c + pltpu.roll(xc, LANES - ROT_DIM // 2, 1) * s1 + pltpu.roll(xc, ROT_DIM // 2, 1) * s2)
    return jnp.concatenate(outs, axis=1)


def _proj_kernel(x_ref, g_ref, w_ref, c_ref, s1_ref, s2_ref,
                 q_ref, kt_ref, ktb_ref, vlin_ref, vb_ref, u_ref):
    h = _rms(x_ref[...], g_ref[...]).astype(BF16)
    proj = jnp.dot(h, w_ref[...], preferred_element_type=F32)
    c, s1, s2 = c_ref[...], s1_ref[...], s2_ref[...]
    q = _rope(proj[:, :ATTN_WIDTH], c, s1, s2)
    k = _rope(proj[:, ATTN_WIDTH:2 * ATTN_WIDTH], c, s1, s2)
    v = proj[:, 2 * ATTN_WIDTH:3 * ATTN_WIDTH]
    q_ref[...] = (q * (1.0 / math.sqrt(HEAD_DIM))).astype(BF16)
    kt = k.T
    kt_ref[0] = kt
    ktb_ref[0] = kt.astype(BF16)
    for hh in range(N_HEADS):
        vlin_ref[pl.ds(hh, v.shape[0], stride=N_HEADS), :] = v[:, hh * LANES:(hh + 1) * LANES]
    vb_ref[...] = v.astype(BF16)
    u_ref[...] = proj[:, 3 * ATTN_WIDTH:]


def _project(x2d, g, w_bf, tabs, n_tab_tiles, rows_per_group):
    m = x2d.shape[0]
    t = TOK_TILE
    groups = m // rows_per_group
    tiles_per_group = rows_per_group // t
    c, s1, s2 = tabs
    tab_spec = pl.BlockSpec((t, LANES), lambda i: (i % n_tab_tiles, 0))
    kt_spec = pl.BlockSpec((1, ATTN_WIDTH, t), lambda i: (i // tiles_per_group, 0, i % tiles_per_group))
    row_spec = pl.BlockSpec((t, ATTN_WIDTH), lambda i: (i, 0))
    return pl.pallas_call(
        _proj_kernel,
        grid=(m // t,),
        in_specs=[pl.BlockSpec((t, D_MODEL), lambda i: (i, 0)),
                  _const_spec((1, D_MODEL)),
                  _const_spec((D_MODEL, 4 * ATTN_WIDTH)),
                  tab_spec, tab_spec, tab_spec],
        out_specs=[row_spec, kt_spec, kt_spec,
                   pl.BlockSpec((t * N_HEADS, LANES), lambda i: (i, 0)),
                   row_spec, row_spec],
        out_shape=[jax.ShapeDtypeStruct((m, ATTN_WIDTH), BF16),
                   jax.ShapeDtypeStruct((groups, ATTN_WIDTH, rows_per_group), F32),
                   jax.ShapeDtypeStruct((groups, ATTN_WIDTH, rows_per_group), BF16),
                   jax.ShapeDtypeStruct((m * N_HEADS, LANES), F32),
                   jax.ShapeDtypeStruct((m, ATTN_WIDTH), BF16),
                   jax.ShapeDtypeStruct((m, SSM_WIDTH), F32)],
        compiler_params=_params(("arbitrary",)),
        name="proj",
    )(x2d, g, w_bf, c, s1, s2)


def _rope_tables(pos):
    half = ROT_DIM // 2
    inv_freq = jnp.float32(ROPE_THETA) ** (-jnp.arange(half, dtype=F32) * 2.0 / ROT_DIM)
    ang = pos.astype(F32)[:, None] * inv_freq[None, :]
    cos, sin = jnp.cos(ang), jnp.sin(ang)
    n = pos.shape[0]
    pad = jnp.zeros((n, HEAD_DIM - ROT_DIM), F32)
    c = jnp.concatenate([cos, cos, pad + 1.0], axis=1)
    s1 = jnp.concatenate([-sin, jnp.zeros_like(sin), pad], axis=1)
    s2 = jnp.concatenate([jnp.zeros_like(sin), sin, pad], axis=1)
    tile2 = lambda a: jnp.concatenate([a, a], axis=1)
    return tile2(c), tile2(s1), tile2(s2)


def _lam(lq1, lk1, lq2, lk2):
    return (jnp.exp(jnp.sum(lq1 * lk1, keepdims=True)) - jnp.exp(jnp.sum(lq2 * lk2, keepdims=True))
            + LAM_INIT)


def _subln(o, gain):
    ms = jnp.mean(o * o, axis=-1, keepdims=True)
    return o * lax.rsqrt(ms + EPS) * gain * (1.0 - LAM_INIT)


def _prompt_attn_kernel(q_ref, kt_ref, v_ref, lq1, lk1, lq2, lk2, gain_ref, o_ref,
                        q2_s, m_s, l_s, acc_s, *, tq):
    qi = pl.program_id(2)
    q = q_ref[...]
    lane = lax.broadcasted_iota(jnp.int32, q.shape, 1)
    zero = jnp.zeros_like(q)
    q2_s[0:tq, :] = jnp.where(lane < HEAD_DIM, q, zero)
    q2_s[tq:2 * tq, :] = jnp.where(lane >= HEAD_DIM, q, zero)
    m_s[...] = jnp.full(m_s.shape, NEG_INF, F32)
    l_s[...] = jnp.zeros(l_s.shape, F32)
    acc_s[...] = jnp.zeros(acc_s.shape, F32)

    def step(ki, masked):
        start = pl.multiple_of(ki * tq, tq)
        kt = kt_ref[0, :, pl.ds(start, tq)]
        v = v_ref[pl.ds(start, tq), :]
        s = jnp.dot(q2_s[...], kt, preferred_element_type=F32)
        if masked:
            row = lax.broadcasted_iota(jnp.int32, s.shape, 0) % tq
            col = lax.broadcasted_iota(jnp.int32, s.shape, 1)
            s = jnp.where(col <= row, s, NEG_INF)
        m_old = m_s[...]
        m_new = jnp.maximum(m_old, jnp.max(s, axis=-1, keepdims=True))
        alpha = jnp.exp(m_old - m_new)
        p = jnp.exp(s - m_new)
        l_s[...] = alpha * l_s[...] + jnp.sum(p, axis=-1, keepdims=True)
        acc_s[...] = alpha * acc_s[...] + jnp.dot(p.astype(BF16), v, preferred_element_type=F32)
        m_s[...] = m_new

    def body(ki, carry):
        step(ki, False)
        return carry

    lax.fori_loop(0, qi, body, 0)
    step(qi, True)

    lam = _lam(lq1[...], lk1[...], lq2[...], lk2[...])
    o = acc_s[...] / l_s[...]
    o = o[0:tq, :] - lam * o[tq:2 * tq, :]
    o_ref[...] = _subln(o, gain_ref[...]).astype(o_ref.dtype)


def _prompt_attention(q_bf, kt_bf, v_bf, lam_params, gain, n_batch, seq, tq=512):
    nq = seq // tq
    lam_spec = _const_spec((1, HEAD_DIM))
    return pl.pallas_call(
        functools.partial(_prompt_attn_kernel, tq=tq),
        grid=(n_batch, N_HEADS, nq),
        in_specs=[pl.BlockSpec((tq, LANES), lambda n, h, i: (n * nq + i, h)),
                  pl.BlockSpec((1, LANES, seq), lambda n, h, i: (n, h, 0)),
                  pl.BlockSpec((seq, LANES), lambda n, h, i: (n, h)),
                  lam_spec, lam_spec, lam_spec, lam_spec,
                  _const_spec((1, LANES))],
        out_specs=pl.BlockSpec((tq, LANES), lambda n, h, i: (n * nq + i, h)),
        out_shape=jax.ShapeDtypeStruct((n_batch * seq, ATTN_WIDTH), BF16),
        scratch_shapes=[pltpu.VMEM((2 * tq, LANES), BF16),
                        pltpu.VMEM((2 * tq, 1), F32),
                        pltpu.VMEM((2 * tq, 1), F32),
                        pltpu.VMEM((2 * tq, LANES), F32)],
        compiler_params=_params(("arbitrary", "arbitrary", "arbitrary")),
        name="prompt_attn",
    )(q_bf, kt_bf, v_bf, *lam_params, gain)


def _paged_attn_kernel(pt_ref, q_ref, knew_ref, vnew_ref, lq1, lk1, lq2, lk2, gain_ref,
                       ck_hbm, cv_hbm, o_ref, kbuf, vbuf, sem, *, n_seq, n_chunks, t_new):
    s_idx = pl.program_id(0)
    ppc = PAGES_PER_CHUNK
    rows = 2 * N_HEADS * t_new

    def page_copies(seq, chunk, slot):
        cps = []
        for pg in range(ppc):
            page = pt_ref[chunk * ppc + pg, seq]
            cps.append(pltpu.make_async_copy(ck_hbm.at[page], kbuf.at[slot, pg], sem.at[0, slot]))
            cps.append(pltpu.make_async_copy(cv_hbm.at[page], vbuf.at[slot, pg], sem.at[1, slot]))
        return cps

    @pl.when(s_idx == 0)
    def _():
        for cp in page_copies(0, 0, 0):
            cp.start()

    q = q_ref[...]
    qt = jnp.concatenate([q] * (2 * N_HEADS), axis=0)
    r_hj = lax.broadcasted_iota(jnp.int32, qt.shape, 0) // t_new
    c_hj = lax.broadcasted_iota(jnp.int32, qt.shape, 1) // HEAD_DIM
    qbd = jnp.where(r_hj == c_hj, qt, 0.0).astype(BF16)

    m = jnp.full((rows, 1), NEG_INF, F32)
    l = jnp.zeros((rows, 1), F32)
    acc = jnp.zeros((rows, ATTN_WIDTH), F32)

    for c in range(n_chunks):
        slot = c % 2
        nslot = (c + 1) % 2
        if c + 1 < n_chunks:
            for cp in page_copies(s_idx, c + 1, nslot):
                cp.start()
        else:
            @pl.when(s_idx + 1 < n_seq)
            def _():
                for cp in page_copies(s_idx + 1, 0, nslot):
                    cp.start()
        for cp in page_copies(s_idx, c, slot):
            cp.wait()
        kt = jnp.concatenate([kbuf[slot, pg].astype(BF16) for pg in range(ppc)], axis=1)
        s = jnp.dot(qbd, kt, preferred_element_type=F32)
        m_new = jnp.maximum(m, jnp.max(s, axis=-1, keepdims=True))
        alpha = jnp.exp(m - m_new)
        p = jnp.exp(s - m_new)
        l = alpha * l + jnp.sum(p, axis=-1, keepdims=True)
        v = jnp.concatenate(
            [jnp.concatenate([vbuf[slot, pg, pl.ds(hh, PAGE_SIZE, stride=N_HEADS), :] for hh in range(N_HEADS)],
                             axis=1) for pg in range(ppc)], axis=0).astype(BF16)
        acc = alpha * acc + jnp.dot(p.astype(BF16), v, preferred_element_type=F32)
        m = m_new

    pad = jnp.zeros((t_new, ATTN_WIDTH), F32)
    knew = jnp.concatenate([knew_ref[...], pad], axis=0).astype(BF16)
    vnew = jnp.concatenate([vnew_ref[...], pad], axis=0).astype(BF16)
    s = lax.dot_general(qbd, knew, (((1,), (1,)), ((), ())), preferred_element_type=F32)
    row_t = lax.broadcasted_iota(jnp.int32, s.shape, 0) % t_new
    col_t = lax.broadcasted_iota(jnp.int32, s.shape, 1)
    s = jnp.where(col_t <= row_t, s, NEG_INF)
    m_new = jnp.maximum(m, jnp.max(s, axis=-1, keepdims=True))
    alpha = jnp.exp(m - m_new)
    p = jnp.exp(s - m_new)
    l = alpha * l + jnp.sum(p, axis=-1, keepdims=True)
    acc = alpha * acc + jnp.dot(p.astype(BF16), vnew, preferred_element_type=F32)
    o_all = acc / l

    lam = _lam(lq1[...], lk1[...], lq2[...], lk2[...])
    gain = gain_ref[...]
    outs = []
    for hh in range(N_HEADS):
        r0 = hh * 2 * t_new
        blk = o_all[r0:r0 + 2 * t_new, hh * LANES:(hh + 1) * LANES]
        o = blk[0:t_new, :] - lam * blk[t_new:2 * t_new, :]
        outs.append(_subln(o, gain))
    o_ref[...] = jnp.concatenate(outs, axis=1)


def _paged_attention(pt_t, q_bf, knew_bf, vnew_bf, lam_params, gain, ck, cv, n_seq, t_new):
    n_pages = pt_t.shape[0]
    n_chunks = n_pages // PAGES_PER_CHUNK
    lam_spec = pl.BlockSpec((1, HEAD_DIM), lambda s, pt: (0, 0))
    row_spec = pl.BlockSpec((t_new, ATTN_WIDTH), lambda s, pt: (s, 0))
    feat = ck.shape[1]
    grid_spec = pltpu.PrefetchScalarGridSpec(
        num_scalar_prefetch=1,
        grid=(n_seq,),
        in_specs=[row_spec, row_spec, row_spec,
                  lam_spec, lam_spec, lam_spec, lam_spec,
                  pl.BlockSpec((1, LANES), lambda s, pt: (0, 0)),
                  pl.BlockSpec(memory_space=pl.ANY),
                  pl.BlockSpec(memory_space=pl.ANY)],
        out_specs=row_spec,
        scratch_shapes=[pltpu.VMEM((2, PAGES_PER_CHUNK, feat, PAGE_SIZE), F32),
                        pltpu.VMEM((2, PAGES_PER_CHUNK, PAGE_SIZE * N_HEADS, LANES), F32),
                        pltpu.SemaphoreType.DMA((2, 2))],
    )
    return pl.pallas_call(
        functools.partial(_paged_attn_kernel, n_seq=n_seq, n_chunks=n_chunks, t_new=t_new),
        grid_spec=grid_spec,
        out_shape=jax.ShapeDtypeStruct((n_seq * t_new, ATTN_WIDTH), F32),
        compiler_params=_params(("arbitrary",)),
        name="paged_attn",
    )(pt_t, q_bf, knew_bf, vnew_bf, *lam_params, gain, ck, cv)


def _s5_kernel(u_ref, h0_ref, bb_ref, cc_ref, a_ref, pw_ref, d_ref, wg_ref, bg_ref,
               ys_ref, ht_ref, bu_s, carry_s, *, chain):
    t = u_ref.shape[0]
    nblk = t // SUBLANES
    u = u_ref[...]
    bu_s[...] = jnp.dot(u.astype(BF16), bb_ref[...], preferred_element_type=F32)

    if chain:
        @pl.when(pl.program_id(1) == 0)
        def _():
            carry_s[...] = h0_ref[0]

    def cmul_add(xr, xi, ar, ai, sr, si):
        return xr + ar * sr - ai * si, xi + ar * si + ai * sr

    def body(b, carry):
        r0 = pl.multiple_of(b * SUBLANES, SUBLANES)
        blk = bu_s[pl.ds(r0, SUBLANES), :]
        xr, xi = blk[:, :N_STATE], blk[:, N_STATE:]
        for k, shift in enumerate((1, 2, 4)):
            sr = pltpu.roll(xr, shift, 0)
            si = pltpu.roll(xi, shift, 0)
            xr, xi = cmul_add(xr, xi, a_ref[k, 0], a_ref[k, 1], sr, si)
        if chain:
            init = carry_s[...]
        else:
            init = h0_ref[pl.ds(b, 1), :]
        cr = jnp.broadcast_to(init[:, :N_STATE], xr.shape)
        ci = jnp.broadcast_to(init[:, N_STATE:], xi.shape)
        xr, xi = cmul_add(xr, xi, pw_ref[0], pw_ref[1], cr, ci)
        x = jnp.concatenate([xr, xi], axis=1)
        bu_s[pl.ds(r0, SUBLANES), :] = x
        last = x[SUBLANES - 1:SUBLANES, :]
        if chain:
            carry_s[...] = last
        else:
            ht_ref[pl.ds(b, 1), :] = last
        return carry

    lax.fori_loop(0, nblk, body, 0)
    if chain:
        ht_ref[0] = carry_s[...]

    y = jnp.dot(bu_s[...].astype(BF16), cc_ref[...], preferred_element_type=F32) + d_ref[...] * u
    g = 0.5 * y * (1.0 + lax.erf(y * (1.0 / math.sqrt(2.0))))
    z = jnp.dot(g.astype(BF16), wg_ref[...], preferred_element_type=F32) + bg_ref[...]
    ys_ref[...] = (g * (1.0 / (1.0 + jnp.exp(-z)))).astype(ys_ref.dtype)


def _s5(u, h0, ssm, chain, n_groups, rows_per_group):
    bb, cc, a_tab, pw_tab, d_row, wg, bg = ssm
    t = TOK_TILE
    m = u.shape[0]
    if chain:
        tiles = rows_per_group // t
        grid = (n_groups, tiles)
        row_map = lambda n, i: (n * tiles + i, 0)
        h_spec = pl.BlockSpec((1, 1, 2 * N_STATE), lambda n, i: (n, 0, 0))
        h_shape = jax.ShapeDtypeStruct((n_groups, 1, 2 * N_STATE), F32)
        const = lambda shape: pl.BlockSpec(shape, lambda n, i: (0,) * len(shape))
        sem = ("arbitrary", "arbitrary")
    else:
        grid = (m // t,)
        row_map = lambda i: (i, 0)
        h_spec = pl.BlockSpec((t // SUBLANES, 2 * N_STATE), lambda i: (i, 0))
        h_shape = jax.ShapeDtypeStruct((m // SUBLANES, 2 * N_STATE), F32)
        const = lambda shape: pl.BlockSpec(shape, lambda i: (0,) * len(shape))
        sem = ("arbitrary",)
    return pl.pallas_call(
        functools.partial(_s5_kernel, chain=chain),
        grid=grid,
        in_specs=[pl.BlockSpec((t, SSM_WIDTH), row_map), h_spec,
                  const(bb.shape), const(cc.shape), const(a_tab.shape), const(pw_tab.shape),
                  const(d_row.shape), const(wg.shape), const(bg.shape)],
        out_specs=[pl.BlockSpec((t, SSM_WIDTH), row_map), h_spec],
        out_shape=[jax.ShapeDtypeStruct((m, SSM_WIDTH), BF16), h_shape],
        scratch_shapes=[pltpu.VMEM((t, 2 * N_STATE), F32), pltpu.VMEM((1, 2 * N_STATE), F32)],
        compiler_params=_params(sem),
        name="s5",
    )(u, h0, bb, cc, a_tab, pw_tab, d_row, wg, bg)


def _ssm_tables(a_re, a_im, log_dt, b_re, b_im, c_re, c_im, d_skip, w_glu, b_glu):
    delta = jnp.exp(log_dt)[:, None]
    mag = jnp.exp(a_re * delta)
    ang = a_im * delta
    lb_re = mag * jnp.cos(ang)
    lb_im = mag * jnp.sin(ang)
    den = a_re * a_re + a_im * a_im
    num_re = lb_re - 1.0
    cz_re = (num_re * a_re + lb_im * a_im) / den
    cz_im = (lb_im * a_re - num_re * a_im) / den
    bb_re = cz_re[..., None] * b_re - cz_im[..., None] * b_im
    bb_im = cz_re[..., None] * b_im + cz_im[..., None] * b_re
    eye = jnp.eye(SSM_GROUPS, dtype=F32)
    bd_in = lambda w: jnp.einsum('gsp,gh->gphs', w, eye).reshape(SSM_WIDTH, N_STATE)
    bd_out = lambda w: jnp.einsum('gps,gh->gshp', w, eye).reshape(N_STATE, SSM_WIDTH)
    bb = jnp.concatenate([bd_in(bb_re), bd_in(bb_im)], axis=1).astype(BF16)
    cc = jnp.concatenate([bd_out(c_re), bd_out(-c_im)], axis=0).astype(BF16)

    def cmul(ar, ai, br, bi):
        return ar * br - ai * bi, ar * bi + ai * br

    l1 = (lb_re.reshape(N_STATE), lb_im.reshape(N_STATE))
    pows = [l1]
    for _ in range(SUBLANES - 1):
        pows.append(cmul(*pows[-1], *l1))
    rows = jnp.arange(SUBLANES)[:, None]
    a_tab = jnp.stack([jnp.stack([jnp.where(rows >= sh, pows[sh - 1][0][None, :], 0.0),
                                  jnp.where(rows >= sh, pows[sh - 1][1][None, :], 0.0)]) for sh in (1, 2, 4)])
    pw_tab = jnp.stack([jnp.stack([p[0] for p in pows]), jnp.stack([p[1] for p in pows])])
    return (bb, cc, a_tab.astype(F32), pw_tab, d_skip.reshape(1, SSM_WIDTH),
            w_glu.astype(BF16), b_glu.reshape(1, SSM_WIDTH))


def _memkv_kernel(mem_ref, g_ref, wk_ref, wv_ref, kt_ref, vt_ref, ktb_ref, vtb_ref):
    mn = _rms(mem_ref[...], g_ref[...]).astype(BF16)
    kt = jnp.dot(mn, wk_ref[...], preferred_element_type=F32).T
    vt = jnp.dot(mn, wv_ref[...], preferred_element_type=F32).T
    kt_ref[0] = kt
    vt_ref[0] = vt
    ktb_ref[0] = kt.astype(BF16)
    vtb_ref[0] = vt.astype(BF16)


def _memory_kv(mem2d, g, wk, wv, n_batch):
    spec = pl.BlockSpec((1, CROSS_WIDTH, N_MEM), lambda n: (n, 0, 0))
    f = jax.ShapeDtypeStruct((n_batch, CROSS_WIDTH, N_MEM), F32)
    b = jax.ShapeDtypeStruct((n_batch, CROSS_WIDTH, N_MEM), BF16)
    return pl.pallas_call(
        _memkv_kernel,
        grid=(n_batch,),
        in_specs=[pl.BlockSpec((N_MEM, D_MODEL), lambda n: (n, 0)), _const_spec((1, D_MODEL)),
                  _const_spec((D_MODEL, CROSS_WIDTH)), _const_spec((D_MODEL, CROSS_WIDTH))],
        out_specs=[spec, spec, spec, spec],
        out_shape=[f, f, b, b],
        compiler_params=_params(("arbitrary",)),
        name="memory_kv",
    )(mem2d, g, wk, wv)


def _cross_attend(hq, kt, vt, n_q):
    heads = CROSS_WIDTH // CROSS_HEAD_DIM
    stacked = jnp.concatenate([hq] * heads, axis=0)
    r_h = lax.broadcasted_iota(jnp.int32, stacked.shape, 0) // n_q
    c_h = lax.broadcasted_iota(jnp.int32, stacked.shape, 1) // CROSS_HEAD_DIM
    own = r_h == c_h
    s = jnp.dot(jnp.where(own, stacked, 0.0).astype(BF16), kt, preferred_element_type=F32)
    p = jnp.exp(s - jnp.max(s, axis=-1, keepdims=True))
    p = p / jnp.sum(p, axis=-1, keepdims=True)
    full = lax.dot_general(p.astype(BF16), vt, (((1,), (1,)), ((), ())), preferred_element_type=F32)
    full = jnp.where(own, full, 0.0)
    out = full[0:n_q, :]
    for hh in range(1, heads):
        out = out + full[hh * n_q:(hh + 1) * n_q, :]
    return out


def _mix_cross_kernel(x_ref, o_ref, ys_ref, wo_ref, g_ref, wq_ref, kt_ref, vt_ref, wco_ref, out_ref,
                      *, per_seq, t_new):
    o = o_ref[...].astype(BF16)
    mix = (jnp.dot(o, wo_ref[0:ATTN_WIDTH, :], preferred_element_type=F32)
           + jnp.dot(ys_ref[...], wo_ref[ATTN_WIDTH:, :], preferred_element_type=F32))
    x1 = x_ref[...] + mix
    hq = jnp.dot(_rms(x1, g_ref[...]).astype(BF16), wq_ref[...], preferred_element_type=F32)
    hq = hq * (1.0 / math.sqrt(CROSS_HEAD_DIM))
    if per_seq:
        pieces = []
        for j in range(x1.shape[0] // t_new):
            pieces.append(_cross_attend(hq[j * t_new:(j + 1) * t_new, :], kt_ref[j].astype(BF16),
                                        vt_ref[j].astype(BF16), t_new))
        oc = jnp.concatenate(pieces, axis=0)
    else:
        oc = _cross_attend(hq, kt_ref[0], vt_ref[0], x1.shape[0])
    out_ref[...] = x1 + jnp.dot(oc.astype(BF16), wco_ref[...], preferred_element_type=F32)


def _mix_cross(x2d, o, ys, wo, g, wq, kt, vt, wco, per_seq, t_new, tiles_per_group, t):
    m = x2d.shape[0]
    if per_seq:
        nseq = t // t_new
        mem_spec = pl.BlockSpec((nseq, CROSS_WIDTH, N_MEM), lambda i: (i, 0, 0))
    else:
        mem_spec = pl.BlockSpec((1, CROSS_WIDTH, N_MEM), lambda i: (i // tiles_per_group, 0, 0))
    return pl.pallas_call(
        functools.partial(_mix_cross_kernel, per_seq=per_seq, t_new=t_new),
        grid=(m // t,),
        in_specs=[pl.BlockSpec((t, D_MODEL), lambda i: (i, 0)),
                  pl.BlockSpec((t, ATTN_WIDTH), lambda i: (i, 0)),
                  pl.BlockSpec((t, SSM_WIDTH), lambda i: (i, 0)),
                  _const_spec((D_MODEL, D_MODEL)), _const_spec((1, D_MODEL)),
                  _const_spec((D_MODEL, CROSS_WIDTH)), mem_spec, mem_spec,
                  _const_spec((CROSS_WIDTH, D_MODEL))],
        out_specs=pl.BlockSpec((t, D_MODEL), lambda i: (i, 0)),
        out_shape=jax.ShapeDtypeStruct((m, D_MODEL), F32),
        compiler_params=_params(("arbitrary",)),
        name="mix_cross",
    )(x2d, o, ys, wo, g, wq, kt, vt, wco)


def _mlp_kernel(x_ref, g_ref, wu_ref, wd_ref, gf_ref, y_ref, *, ff_chunk):
    x = x_ref[...]
    h = _rms(x, g_ref[...]).astype(BF16)
    acc = x
    for c in range(D_FF // ff_chunk):
        z = jnp.dot(h, wu_ref[:, c * ff_chunk:(c + 1) * ff_chunk], preferred_element_type=F32)
        a = jnp.square(jnp.maximum(z, 0.0)).astype(BF16)
        acc = acc + jnp.dot(a, wd_ref[c * ff_chunk:(c + 1) * ff_chunk, :], preferred_element_type=F32)
    y_ref[...] = _rms(acc, gf_ref[...])


def _mlp(x2d, g, wu, wd, gf, ff_chunk=1024):
    m = x2d.shape[0]
    t = TOK_TILE
    return pl.pallas_call(
        functools.partial(_mlp_kernel, ff_chunk=ff_chunk),
        grid=(m // t,),
        in_specs=[pl.BlockSpec((t, D_MODEL), lambda i: (i, 0)), _const_spec((1, D_MODEL)),
                  _const_spec((D_MODEL, D_FF)), _const_spec((D_FF, D_MODEL)), _const_spec((1, D_MODEL))],
        out_specs=pl.BlockSpec((t, D_MODEL), lambda i: (i, 0)),
        out_shape=jax.ShapeDtypeStruct((m, D_MODEL), F32),
        compiler_params=_params(("arbitrary",)),
        name="mlp",
    )(x2d, g, wu, wd, gf)


def kernel(x_prompt, x_sample, mem_prompt, cache_k, cache_v, page_table, state_ssm_re, state_ssm_im, cache_mem_k, cache_mem_v, norm_mix, w_in, lambda_q1, lambda_k1, lambda_q2, lambda_k2, subln_gain, ssm_a_re, ssm_a_im, ssm_log_dt, ssm_b_re, ssm_b_im, ssm_c_re, ssm_c_im, ssm_d, w_glu, b_glu, w_out, norm_cross, norm_mem, w_cq, w_ck, w_cv, w_co, norm_mlp, w_up, w_down, final_norm):
    n_p, t_p = x_prompt.shape[0], x_prompt.shape[1]
    n_s, t_s = x_sample.shape[0], x_sample.shape[1]
    n_pool = cache_k.shape[1]
    past = page_table.shape[1] * PAGE_SIZE
    assert cache_k.shape[0] == 1 and t_s == SUBLANES and t_p % TOK_TILE == 0 and (n_s * t_s) % TOK_TILE == 0

    l = 0
    w_in_b = w_in[l].astype(BF16)
    w_out_b = w_out[l].astype(BF16)
    w_cq_b, w_ck_b, w_cv_b, w_co_b = (w[l].astype(BF16) for w in (w_cq, w_ck, w_cv, w_co))
    w_up_b, w_down_b = w_up[l].astype(BF16), w_down[l].astype(BF16)
    lam_params = (lambda_q1, lambda_k1, lambda_q2, lambda_k2)
    ssm = _ssm_tables(ssm_a_re[l], ssm_a_im[l], ssm_log_dt[l], ssm_b_re[l], ssm_b_im[l],
                      ssm_c_re[l], ssm_c_im[l], ssm_d[l], w_glu[l], b_glu[l])
    final_g = final_norm.reshape(1, D_MODEL)

    xp = x_prompt.reshape(n_p * t_p, D_MODEL)
    tabs_p = _rope_tables(jnp.arange(t_p, dtype=jnp.int32))
    q_p, kt_p, ktb_p, vlin_p, vb_p, u_p = _project(xp, norm_mix, w_in_b, tabs_p, t_p // TOK_TILE, t_p)
    o_p = _prompt_attention(q_p, ktb_p, vb_p, lam_params, subln_gain, n_p, t_p)
    h0_p = jnp.zeros((n_p, 1, 2 * N_STATE), F32)
    ys_p, ht_p = _s5(u_p, h0_p, ssm, True, n_p, t_p)
    mkt, mvt, mktb, mvtb = _memory_kv(mem_prompt.reshape(n_p * N_MEM, D_MODEL), norm_mem, w_ck_b, w_cv_b, n_p)
    x2_p = _mix_cross(xp, o_p, ys_p, w_out_b, norm_cross, w_cq_b, mktb, mvtb, w_co_b,
                      False, t_s, t_p // TOK_TILE, TOK_TILE)
    y_p = _mlp(x2_p, norm_mlp, w_up_b, w_down_b, final_g)

    xs = x_sample.reshape(n_s * t_s, D_MODEL)
    pos_s = past + (jnp.arange(TOK_TILE, dtype=jnp.int32) % t_s)
    tabs_s = _rope_tables(pos_s)
    q_s, kt_s, ktb_s, vlin_s, vb_s, u_s = _project(xs, norm_mix, w_in_b, tabs_s, 1, n_s * t_s)
    k_s = kt_s[0].T
    ck = jnp.transpose(cache_k, (0, 1, 3, 4, 5, 2)).reshape(n_pool, ATTN_WIDTH, PAGE_SIZE)
    cv = cache_v.reshape(n_pool, PAGE_SIZE * N_HEADS, LANES)
    o_s = _paged_attention(page_table.T, q_s.astype(F32), k_s, vb_s.astype(F32), lam_params, subln_gain,
                           ck, cv, n_s, t_s)
    h0_s = jnp.concatenate([state_ssm_re[l].reshape(n_s, N_STATE), state_ssm_im[l].reshape(n_s, N_STATE)], axis=1)
    ys_s, ht_s = _s5(u_s, h0_s, ssm, False, 1, n_s * t_s)
    cmk = jnp.transpose(cache_mem_k[l], (0, 2, 3, 1)).reshape(n_s, CROSS_WIDTH, N_MEM)
    cmv = jnp.transpose(cache_mem_v[l], (0, 2, 3, 1)).reshape(n_s, CROSS_WIDTH, N_MEM)
    x2_s = _mix_cross(xs, o_s, ys_s, w_out_b, norm_cross, w_cq_b, cmk, cmv, w_co_b, True, t_s, 1, 128)
    y_s = _mlp(x2_s, norm_mlp, w_up_b, w_down_b, final_g)

    y_prompt = y_p.reshape(n_p, t_p, D_MODEL)
    y_sample = y_s.reshape(n_s, t_s, D_MODEL)
    k_prompt = jnp.transpose(kt_p.reshape(1, n_p, N_HEADS, 2, HEAD_DIM, t_p), (0, 1, 5, 2, 3, 4))
    v_prompt = vlin_p.reshape(1, n_p, t_p, N_HEADS, 2 * HEAD_DIM)
    ht_p = ht_p.reshape(n_p, 2 * N_STATE)
    ssm_re_p = ht_p[:, :N_STATE].reshape(1, n_p, SSM_GROUPS, SSM_STATE)
    ssm_im_p = ht_p[:, N_STATE:].reshape(1, n_p, SSM_GROUPS, SSM_STATE)
    unpack_mem = lambda a: jnp.transpose(a.reshape(1, n_p, CROSS_WIDTH // CROSS_HEAD_DIM, CROSS_HEAD_DIM, N_MEM),
                                         (0, 1, 4, 2, 3))
    k_sample = k_s.reshape(1, n_s, t_s, N_HEADS, 2, HEAD_DIM)
    v_sample = vlin_s.reshape(1, n_s, t_s, N_HEADS, 2 * HEAD_DIM)
    ssm_re_s = ht_s[:, :N_STATE].reshape(1, n_s, SSM_GROUPS, SSM_STATE)
    ssm_im_s = ht_s[:, N_STATE:].reshape(1, n_s, SSM_GROUPS, SSM_STATE)
    return (y_prompt, y_sample, k_prompt, v_prompt, ssm_re_p, ssm_im_p, unpack_mem(mkt), unpack_mem(mvt),
            k_sample, v_sample, ssm_re_s, ssm_im_s)
```

```python
import functools
import math

import jax
import jax.numpy as jnp
from jax import lax
from jax.experimental import pallas as pl
from jax.experimental.pallas import tpu as pltpu

F32 = jnp.float32
BF16 = jnp.bfloat16

D_MODEL = 1024
HEAD_DIM = 64
N_HEADS = 4
ATTN_WIDTH = 512
ROT_DIM = 16
ROPE_THETA = 500000.0
SSM_WIDTH = 512
SSM_GROUP = 16
SSM_GROUPS = 32
SSM_STATE = 64
N_STATE = SSM_GROUPS * SSM_STATE
PAGE_SIZE = 128
N_MEM = 256
CROSS_WIDTH = 256
CROSS_HEAD_DIM = 64
D_FF = 4096
EPS = 1e-6
NEG_INF = -1e30
LAM_INIT = 0.8 - 0.6 * math.exp(-0.3 * 0)

LANES = 128
SUBLANES = 8
VMEM_LIMIT = 56 * 1024 * 1024

TOK_TILE = 256
PAGES_PER_CHUNK = 16


def _params(sem):
    return pltpu.CompilerParams(dimension_semantics=sem, vmem_limit_bytes=VMEM_LIMIT)


def _rms(x, g):
    ms = jnp.mean(x * x, axis=-1, keepdims=True)
    return x * lax.rsqrt(ms + EPS) * g


def _const_spec(shape):
    nd = len(shape)
    return pl.BlockSpec(shape, lambda *_: (0,) * nd)


def _rope(x, c, s1, s2):
    outs = []
    for i in range(x.shape[1] // LANES):
        xc = x[:, i * LANES:(i + 1) * LANES]
        outs.append(xc * c + pltpu.roll(xc, LANES - ROT_DIM // 2, 1) * s1 + pltpu.roll(xc, ROT_DIM // 2, 1) * s2)
    return jnp.concatenate(outs, axis=1)


def _proj_kernel(x_ref, g_ref, w_ref, c_ref, s1_ref, s2_ref, *out_refs, sample):
    h = _rms(x_ref[...], g_ref[...]).astype(BF16)
    proj = jnp.dot(h, w_ref[...], preferred_element_type=F32)
    c, s1, s2 = c_ref[...], s1_ref[...], s2_ref[...]
    q = _rope(proj[:, :ATTN_WIDTH], c, s1, s2) * (1.0 / math.sqrt(HEAD_DIM))
    k = _rope(proj[:, ATTN_WIDTH:2 * ATTN_WIDTH], c, s1, s2)
    v = proj[:, 2 * ATTN_WIDTH:3 * ATTN_WIDTH]
    if sample:
        q_ref, k_ref, v_ref, vlin_ref, u_ref = out_refs
        q_ref[...] = q
        k_ref[...] = k
        v_ref[...] = v
    else:
        qt_ref, kb_ref, kt_ref, vt_ref, vlin_ref, u_ref = out_refs
        qt_ref[0] = q.T.astype(BF16)
        kb_ref[...] = k.astype(BF16)
        kt_ref[0] = k.T
        vt_ref[0] = v.T.astype(BF16)
    for hh in range(N_HEADS):
        vlin_ref[pl.ds(hh, v.shape[0], stride=N_HEADS), :] = v[:, hh * LANES:(hh + 1) * LANES]
    u_ref[...] = proj[:, 3 * ATTN_WIDTH:]


def _project(x2d, g, w_bf, tabs, n_tab_tiles, rows_per_group, sample):
    m = x2d.shape[0]
    t = TOK_TILE
    groups = m // rows_per_group
    tiles_per_group = rows_per_group // t
    c, s1, s2 = tabs
    tab_spec = pl.BlockSpec((t, LANES), lambda i: (i % n_tab_tiles, 0))
    ft_spec = pl.BlockSpec((1, ATTN_WIDTH, t), lambda i: (i // tiles_per_group, 0, i % tiles_per_group))
    row_spec = pl.BlockSpec((t, ATTN_WIDTH), lambda i: (i, 0))
    lin_spec = pl.BlockSpec((t * N_HEADS, LANES), lambda i: (i, 0))
    row = lambda dt: jax.ShapeDtypeStruct((m, ATTN_WIDTH), dt)
    ft = lambda dt: jax.ShapeDtypeStruct((groups, ATTN_WIDTH, rows_per_group), dt)
    lin = jax.ShapeDtypeStruct((m * N_HEADS, LANES), F32)
    if sample:
        out_specs = [row_spec, row_spec, row_spec, lin_spec, row_spec]
        out_shape = [row(F32), row(F32), row(F32), lin, row(F32)]
    else:
        out_specs = [ft_spec, row_spec, ft_spec, ft_spec, lin_spec, row_spec]
        out_shape = [ft(BF16), row(BF16), ft(F32), ft(BF16), lin, row(F32)]
    return pl.pallas_call(
        functools.partial(_proj_kernel, sample=sample),
        grid=(m // t,),
        in_specs=[pl.BlockSpec((t, D_MODEL), lambda i: (i, 0)),
                  _const_spec((1, D_MODEL)),
                  _const_spec((D_MODEL, 4 * ATTN_WIDTH)),
                  tab_spec, tab_spec, tab_spec],
        out_specs=out_specs,
        out_shape=out_shape,
        compiler_params=_params(("arbitrary",)),
        name="proj",
    )(x2d, g, w_bf, c, s1, s2)


def _rope_tables(pos):
    half = ROT_DIM // 2
    inv_freq = jnp.float32(ROPE_THETA) ** (-jnp.arange(half, dtype=F32) * 2.0 / ROT_DIM)
    ang = pos.astype(F32)[:, None] * inv_freq[None, :]
    cos, sin = jnp.cos(ang), jnp.sin(ang)
    n = pos.shape[0]
    pad = jnp.zeros((n, HEAD_DIM - ROT_DIM), F32)
    c = jnp.concatenate([cos, cos, pad + 1.0], axis=1)
    s1 = jnp.concatenate([-sin, jnp.zeros_like(sin), pad], axis=1)
    s2 = jnp.concatenate([jnp.zeros_like(sin), sin, pad], axis=1)
    tile2 = lambda a: jnp.concatenate([a, a], axis=1)
    return tile2(c), tile2(s1), tile2(s2)


def _lam(lq1, lk1, lq2, lk2):
    return (jnp.exp(jnp.sum(lq1 * lk1, keepdims=True)) - jnp.exp(jnp.sum(lq2 * lk2, keepdims=True))
            + LAM_INIT)


def _subln(o, gain):
    ms = jnp.mean(o * o, axis=-1, keepdims=True)
    return o * lax.rsqrt(ms + EPS) * gain * (1.0 - LAM_INIT)


def _prompt_attn_kernel(qt_ref, k_ref, vt_ref, lq1, lk1, lq2, lk2, gain_ref, o_ref,
                        q2_s, m_s, l_s, acc_s, *, tq):
    qi = pl.program_id(2)
    qt = qt_ref[0]
    feat = lax.broadcasted_iota(jnp.int32, qt.shape, 0)
    zero = jnp.zeros_like(qt)
    q2_s[:, 0:tq] = jnp.where(feat < HEAD_DIM, qt, zero)
    q2_s[:, tq:2 * tq] = jnp.where(feat >= HEAD_DIM, qt, zero)
    m_s[...] = jnp.full(m_s.shape, NEG_INF, F32)
    l_s[...] = jnp.zeros(l_s.shape, F32)
    acc_s[...] = jnp.zeros(acc_s.shape, F32)

    def step(ki, masked):
        start = pl.multiple_of(ki * tq, tq)
        k = k_ref[pl.ds(start, tq), :]
        s = jnp.dot(k, q2_s[...], preferred_element_type=F32)
        if masked:
            key = lax.broadcasted_iota(jnp.int32, s.shape, 0)
            qry = lax.broadcasted_iota(jnp.int32, s.shape, 1) % tq
            s = jnp.where(key <= qry, s, NEG_INF)
        m_old = m_s[...]
        m_new = jnp.maximum(m_old, jnp.max(s, axis=0, keepdims=True))
        alpha = jnp.exp(m_old - m_new)
        p = jnp.exp(s - m_new)
        l_s[...] = alpha * l_s[...] + jnp.sum(p, axis=0, keepdims=True)
        vt = vt_ref[0, :, pl.ds(start, tq)]
        acc_s[...] = alpha * acc_s[...] + jnp.dot(vt, p.astype(BF16), preferred_element_type=F32)
        m_s[...] = m_new

    def body(ki, carry):
        step(ki, False)
        return carry

    lax.fori_loop(0, qi, body, 0)
    step(qi, True)

    lam = _lam(lq1[...], lk1[...], lq2[...], lk2[...])
    o = acc_s[...] / l_s[...]
    o = o[:, 0:tq] - lam * o[:, tq:2 * tq]
    ms = jnp.mean(o * o, axis=0, keepdims=True)
    o = o * lax.rsqrt(ms + EPS) * gain_ref[...] * (1.0 - LAM_INIT)
    o_ref[...] = o.T.astype(o_ref.dtype)


def _prompt_attention(qt_bf, k_bf, vt_bf, lam_params, gain_col, n_batch, seq, tq=512):
    nq = seq // tq
    lam_spec = _const_spec((1, HEAD_DIM))
    return pl.pallas_call(
        functools.partial(_prompt_attn_kernel, tq=tq),
        grid=(n_batch, N_HEADS, nq),
        in_specs=[pl.BlockSpec((1, LANES, tq), lambda n, h, i: (n, h, i)),
                  pl.BlockSpec((seq, LANES), lambda n, h, i: (n, h)),
                  pl.BlockSpec((1, LANES, seq), lambda n, h, i: (n, h, 0)),
                  lam_spec, lam_spec, lam_spec, lam_spec,
                  _const_spec((LANES, 1))],
        out_specs=pl.BlockSpec((tq, LANES), lambda n, h, i: (n * nq + i, h)),
        out_shape=jax.ShapeDtypeStruct((n_batch * seq, ATTN_WIDTH), BF16),
        scratch_shapes=[pltpu.VMEM((LANES, 2 * tq), BF16),
                        pltpu.VMEM((1, 2 * tq), F32),
                        pltpu.VMEM((1, 2 * tq), F32),
                        pltpu.VMEM((LANES, 2 * tq), F32)],
        compiler_params=_params(("arbitrary", "arbitrary", "arbitrary")),
        name="prompt_attn",
    )(qt_bf, k_bf, vt_bf, *lam_params, gain_col)


def _paged_attn_kernel(pt_ref, q_ref, knew_ref, vnew_ref, lq1, lk1, lq2, lk2, gain_ref,
                       ck_hbm, cv_hbm, o_ref, kbuf, vbuf, sem, *, n_seq, n_chunks, t_new):
    s_idx = pl.program_id(0)
    ppc = PAGES_PER_CHUNK
    rows = 2 * N_HEADS * t_new

    def page_copies(seq, chunk, slot):
        cps = []
        for pg in range(ppc):
            page = pt_ref[chunk * ppc + pg, seq]
            cps.append(pltpu.make_async_copy(ck_hbm.at[page], kbuf.at[slot, pg], sem.at[0, slot]))
            cps.append(pltpu.make_async_copy(cv_hbm.at[page], vbuf.at[slot, pg], sem.at[1, slot]))
        return cps

    @pl.when(s_idx == 0)
    def _():
        for cp in page_copies(0, 0, 0):
            cp.start()

    q = q_ref[...]
    qt = jnp.concatenate([q] * (2 * N_HEADS), axis=0)
    r_hj = lax.broadcasted_iota(jnp.int32, qt.shape, 0) // t_new
    c_hj = lax.broadcasted_iota(jnp.int32, qt.shape, 1) // HEAD_DIM
    qbd = jnp.where(r_hj == c_hj, qt, 0.0).astype(BF16)

    m = jnp.full((rows, 1), NEG_INF, F32)
    l = jnp.zeros((rows, 1), F32)
    acc = jnp.zeros((rows, ATTN_WIDTH), F32)

    for c in range(n_chunks):
        slot = c % 2
        nslot = (c + 1) % 2
        if c + 1 < n_chunks:
            for cp in page_copies(s_idx, c + 1, nslot):
                cp.start()
        else:
            @pl.when(s_idx + 1 < n_seq)
            def _():
                for cp in page_copies(s_idx + 1, 0, nslot):
                    cp.start()
        for cp in page_copies(s_idx, c, slot):
            cp.wait()
        kt = jnp.concatenate([kbuf[slot, pg].astype(BF16) for pg in range(ppc)], axis=1)
        s = jnp.dot(qbd, kt, preferred_element_type=F32)
        m_new = jnp.maximum(m, jnp.max(s, axis=-1, keepdims=True))
        alpha = jnp.exp(m - m_new)
        p = jnp.exp(s - m_new)
        l = alpha * l + jnp.sum(p, axis=-1, keepdims=True)
        v = jnp.concatenate(
            [jnp.concatenate([vbuf[slot, pg, pl.ds(hh, PAGE_SIZE, stride=N_HEADS), :] for hh in range(N_HEADS)],
                             axis=1) for pg in range(ppc)], axis=0).astype(BF16)
        acc = alpha * acc + jnp.dot(p.astype(BF16), v, preferred_element_type=F32)
        m = m_new

    pad = jnp.zeros((t_new, ATTN_WIDTH), F32)
    knew = jnp.concatenate([knew_ref[...], pad], axis=0).astype(BF16)
    vnew = jnp.concatenate([vnew_ref[...], pad], axis=0).astype(BF16)
    s = lax.dot_general(qbd, knew, (((1,), (1,)), ((), ())), preferred_element_type=F32)
    row_t = lax.broadcasted_iota(jnp.int32, s.shape, 0) % t_new
    col_t = lax.broadcasted_iota(jnp.int32, s.shape, 1)
    s = jnp.where(col_t <= row_t, s, NEG_INF)
    m_new = jnp.maximum(m, jnp.max(s, axis=-1, keepdims=True))
    alpha = jnp.exp(m - m_new)
    p = jnp.exp(s - m_new)
    l = alpha * l + jnp.sum(p, axis=-1, keepdims=True)
    acc = alpha * acc + jnp.dot(p.astype(BF16), vnew, preferred_element_type=F32)
    o_all = acc / l

    lam = _lam(lq1[...], lk1[...], lq2[...], lk2[...])
    gain = gain_ref[...]
    outs = []
    for hh in range(N_HEADS):
        r0 = hh * 2 * t_new
        blk = o_all[r0:r0 + 2 * t_new, hh * LANES:(hh + 1) * LANES]
        o = blk[0:t_new, :] - lam * blk[t_new:2 * t_new, :]
        outs.append(_subln(o, gain))
    o_ref[...] = jnp.concatenate(outs, axis=1)


def _paged_attention(pt_t, q, knew, vnew, lam_params, gain, ck, cv, n_seq, t_new):
    n_pages = pt_t.shape[0]
    n_chunks = n_pages // PAGES_PER_CHUNK
    lam_spec = pl.BlockSpec((1, HEAD_DIM), lambda s, pt: (0, 0))
    row_spec = pl.BlockSpec((t_new, ATTN_WIDTH), lambda s, pt: (s, 0))
    feat = ck.shape[1]
    grid_spec = pltpu.PrefetchScalarGridSpec(
        num_scalar_prefetch=1,
        grid=(n_seq,),
        in_specs=[row_spec, row_spec, row_spec,
                  lam_spec, lam_spec, lam_spec, lam_spec,
                  pl.BlockSpec((1, LANES), lambda s, pt: (0, 0)),
                  pl.BlockSpec(memory_space=pl.ANY),
                  pl.BlockSpec(memory_space=pl.ANY)],
        out_specs=row_spec,
        scratch_shapes=[pltpu.VMEM((2, PAGES_PER_CHUNK, feat, PAGE_SIZE), F32),
                        pltpu.VMEM((2, PAGES_PER_CHUNK, PAGE_SIZE * N_HEADS, LANES), F32),
                        pltpu.SemaphoreType.DMA((2, 2))],
    )
    return pl.pallas_call(
        functools.partial(_paged_attn_kernel, n_seq=n_seq, n_chunks=n_chunks, t_new=t_new),
        grid_spec=grid_spec,
        out_shape=jax.ShapeDtypeStruct((n_seq * t_new, ATTN_WIDTH), F32),
        compiler_params=_params(("arbitrary",)),
        name="paged_attn",
    )(pt_t, q, knew, vnew, *lam_params, gain, ck, cv)


HALF_STATE = N_STATE // 2
S5_TIME_TILE = 32


def _s5_tail(x_bf, u, cc_ref, d_ref, wg_ref, bg_ref):
    y = jnp.concatenate(
        [jnp.dot(x_bf[:, hf * N_STATE:(hf + 1) * N_STATE], cc_ref[hf], preferred_element_type=F32) for hf in range(2)],
        axis=1) + d_ref[...] * u
    g = 0.5 * y * (1.0 + lax.erf(y * (1.0 / math.sqrt(2.0))))
    z = jnp.dot(g.astype(BF16), wg_ref[...], preferred_element_type=F32) + bg_ref[...]
    return g * (1.0 / (1.0 + jnp.exp(-z)))


def _s5_bu(u_bf, bb_ref, bu_s):
    half_u = SSM_WIDTH // 2
    for hf in range(2):
        bu_s[:, hf * N_STATE:(hf + 1) * N_STATE] = jnp.dot(u_bf[:, hf * half_u:(hf + 1) * half_u], bb_ref[hf],
                                                           preferred_element_type=F32)


def _s5_prompt_kernel(u_ref, perm_ref, permt_ref, bb_ref, cc_ref, lb_ref, d_ref, wg_ref, bg_ref,
                      ys_ref, ht_ref, bu_s, carry_s):
    nb, tt = u_ref.shape[0], u_ref.shape[1]
    rows = nb * tt

    @pl.when(pl.program_id(0) == 0)
    def _():
        carry_s[...] = jnp.zeros(carry_s.shape, F32)

    u = u_ref[...].reshape(rows, SSM_WIDTH)
    hi = u.astype(BF16)
    r1 = u - hi.astype(F32)
    mid = r1.astype(BF16)
    lo = (r1 - mid.astype(F32)).astype(BF16)
    perm = perm_ref[...]
    u_hi = jnp.dot(perm, hi, preferred_element_type=F32)
    u_tb = u_hi + jnp.dot(perm, mid, preferred_element_type=F32) + jnp.dot(perm, lo, preferred_element_type=F32)
    _s5_bu(u_hi.astype(BF16), bb_ref, bu_s)

    for hf in range(2):
        c0 = hf * N_STATE
        lbr = lb_ref[0, :, hf * HALF_STATE:(hf + 1) * HALF_STATE]
        lbi = lb_ref[1, :, hf * HALF_STATE:(hf + 1) * HALF_STATE]

        def body(t, carry, c0=c0, lbr=lbr, lbi=lbi):
            xr, xi = carry
            r0 = pl.multiple_of(t * nb, nb)
            br = bu_s[pl.ds(r0, nb), c0:c0 + HALF_STATE]
            bi = bu_s[pl.ds(r0, nb), c0 + HALF_STATE:c0 + N_STATE]
            xr, xi = lbr * xr - lbi * xi + br, lbr * xi + lbi * xr + bi
            bu_s[pl.ds(r0, nb), c0:c0 + HALF_STATE] = xr
            bu_s[pl.ds(r0, nb), c0 + HALF_STATE:c0 + N_STATE] = xi
            return xr, xi

        xr, xi = lax.fori_loop(0, tt, body, (carry_s[:, c0:c0 + HALF_STATE], carry_s[:, c0 + HALF_STATE:c0 + N_STATE]),
                               unroll=4)
        carry_s[:, c0:c0 + HALF_STATE] = xr
        carry_s[:, c0 + HALF_STATE:c0 + N_STATE] = xi
    ht_ref[...] = carry_s[...]

    ys_tb = _s5_tail(bu_s[...].astype(BF16), u_tb, cc_ref, d_ref, wg_ref, bg_ref).astype(BF16)
    ys = jnp.dot(permt_ref[...], ys_tb, preferred_element_type=F32)
    ys_ref[...] = ys.astype(ys_ref.dtype).reshape(nb, tt, SSM_WIDTH)


def _s5_prompt(u3, ssm):
    bb, cc, lb_tab, a_tab, pw_tab, d_row, wg, bg = ssm
    nb, seq = u3.shape[0], u3.shape[1]
    tt = S5_TIME_TILE
    rows = nb * tt
    r = jnp.arange(rows)
    perm = jax.nn.one_hot((r % nb) * tt + r // nb, rows, dtype=BF16)
    blk = pl.BlockSpec((nb, tt, SSM_WIDTH), lambda i: (0, i, 0))
    return pl.pallas_call(
        _s5_prompt_kernel,
        grid=(seq // tt,),
        in_specs=[blk, _const_spec((rows, rows)), _const_spec((rows, rows)),
                  _const_spec(bb.shape), _const_spec(cc.shape), _const_spec(lb_tab.shape),
                  _const_spec(d_row.shape), _const_spec(wg.shape), _const_spec(bg.shape)],
        out_specs=[blk, _const_spec((nb, 2 * N_STATE))],
        out_shape=[jax.ShapeDtypeStruct((nb, seq, SSM_WIDTH), BF16),
                   jax.ShapeDtypeStruct((nb, 2 * N_STATE), F32)],
        scratch_shapes=[pltpu.VMEM((rows, 2 * N_STATE), F32), pltpu.VMEM((nb, 2 * N_STATE), F32)],
        compiler_params=_params(("arbitrary",)),
        name="s5_prompt",
    )(u3, perm, perm.T, bb, cc, lb_tab, d_row, wg, bg)


def _s5_sample_kernel(u_ref, h0_ref, bb_ref, cc_ref, a_ref, pw_ref, d_ref, wg_ref, bg_ref,
                      ys_ref, ht_ref, bu_s):
    t = u_ref.shape[0]
    u = u_ref[...]
    _s5_bu(u.astype(BF16), bb_ref, bu_s)

    def cmul_add(xr, xi, ar, ai, sr, si):
        return xr + ar * sr - ai * si, xi + ar * si + ai * sr

    def body(b, carry):
        r0 = pl.multiple_of(b * SUBLANES, SUBLANES)
        init = h0_ref[pl.ds(b, 1), :]
        for hf in range(2):
            c0 = hf * N_STATE
            st = slice(hf * HALF_STATE, (hf + 1) * HALF_STATE)
            xr = bu_s[pl.ds(r0, SUBLANES), c0:c0 + HALF_STATE]
            xi = bu_s[pl.ds(r0, SUBLANES), c0 + HALF_STATE:c0 + N_STATE]
            for k, shift in enumerate((1, 2, 4)):
                sr = pltpu.roll(xr, shift, 0)
                si = pltpu.roll(xi, shift, 0)
                xr, xi = cmul_add(xr, xi, a_ref[k, 0, :, st], a_ref[k, 1, :, st], sr, si)
            cr = jnp.broadcast_to(init[:, c0:c0 + HALF_STATE], xr.shape)
            ci = jnp.broadcast_to(init[:, c0 + HALF_STATE:c0 + N_STATE], xi.shape)
            xr, xi = cmul_add(xr, xi, pw_ref[0, :, st], pw_ref[1, :, st], cr, ci)
            bu_s[pl.ds(r0, SUBLANES), c0:c0 + HALF_STATE] = xr
            bu_s[pl.ds(r0, SUBLANES), c0 + HALF_STATE:c0 + N_STATE] = xi
            ht_ref[pl.ds(b, 1), c0:c0 + HALF_STATE] = xr[SUBLANES - 1:SUBLANES, :]
            ht_ref[pl.ds(b, 1), c0 + HALF_STATE:c0 + N_STATE] = xi[SUBLANES - 1:SUBLANES, :]
        return carry

    lax.fori_loop(0, t // SUBLANES, body, 0)
    ys_ref[...] = _s5_tail(bu_s[...].astype(BF16), u, cc_ref, d_ref, wg_ref, bg_ref).astype(ys_ref.dtype)


def _s5_sample(u, h0, ssm):
    bb, cc, lb_tab, a_tab, pw_tab, d_row, wg, bg = ssm
    t = TOK_TILE
    m = u.shape[0]
    h_spec = pl.BlockSpec((t // SUBLANES, 2 * N_STATE), lambda i: (i, 0))
    return pl.pallas_call(
        _s5_sample_kernel,
        grid=(m // t,),
        in_specs=[pl.BlockSpec((t, SSM_WIDTH), lambda i: (i, 0)), h_spec,
                  _const_spec(bb.shape), _const_spec(cc.shape), _const_spec(a_tab.shape), _const_spec(pw_tab.shape),
                  _const_spec(d_row.shape), _const_spec(wg.shape), _const_spec(bg.shape)],
        out_specs=[pl.BlockSpec((t, SSM_WIDTH), lambda i: (i, 0)), h_spec],
        out_shape=[jax.ShapeDtypeStruct((m, SSM_WIDTH), BF16),
                   jax.ShapeDtypeStruct((m // SUBLANES, 2 * N_STATE), F32)],
        scratch_shapes=[pltpu.VMEM((t, 2 * N_STATE), F32)],
        compiler_params=_params(("arbitrary",)),
        name="s5_sample",
    )(u, h0, bb, cc, a_tab, pw_tab, d_row, wg, bg)


def _state_pack(re, im):
    return jnp.concatenate([re[:, :HALF_STATE], im[:, :HALF_STATE], re[:, HALF_STATE:], im[:, HALF_STATE:]], axis=1)


def _state_unpack(x):
    re = jnp.concatenate([x[:, 0:HALF_STATE], x[:, N_STATE:N_STATE + HALF_STATE]], axis=1)
    im = jnp.concatenate([x[:, HALF_STATE:N_STATE], x[:, N_STATE + HALF_STATE:]], axis=1)
    return re, im


def _ssm_tables(a_re, a_im, log_dt, b_re, b_im, c_re, c_im, d_skip, w_glu, b_glu):
    delta = jnp.exp(log_dt)[:, None]
    mag = jnp.exp(a_re * delta)
    ang = a_im * delta
    lb_re = mag * jnp.cos(ang)
    lb_im = mag * jnp.sin(ang)
    den = a_re * a_re + a_im * a_im
    num_re = lb_re - 1.0
    cz_re = (num_re * a_re + lb_im * a_im) / den
    cz_im = (lb_im * a_re - num_re * a_im) / den
    bb_re = cz_re[..., None] * b_re - cz_im[..., None] * b_im
    bb_im = cz_re[..., None] * b_im + cz_im[..., None] * b_re
    gh = SSM_GROUPS // 2
    eye = jnp.eye(gh, dtype=F32)
    bd_in = lambda w: jnp.einsum('gsp,gh->gphs', w, eye).reshape(gh * SSM_GROUP, HALF_STATE)
    bd_out = lambda w: jnp.einsum('gps,gh->gshp', w, eye).reshape(HALF_STATE, gh * SSM_GROUP)
    halves = lambda w: (w[:gh], w[gh:])
    bb = jnp.stack([jnp.concatenate([bd_in(r), bd_in(i)], axis=1)
                    for r, i in zip(halves(bb_re), halves(bb_im))]).astype(BF16)
    cc = jnp.stack([jnp.concatenate([bd_out(r), bd_out(-i)], axis=0)
                    for r, i in zip(halves(c_re), halves(c_im))]).astype(BF16)

    def cmul(ar, ai, br, bi):
        return ar * br - ai * bi, ar * bi + ai * br

    l1 = (lb_re.reshape(N_STATE), lb_im.reshape(N_STATE))
    pows = [l1]
    for _ in range(SUBLANES - 1):
        pows.append(cmul(*pows[-1], *l1))
    rows = jnp.arange(SUBLANES)[:, None]
    a_tab = jnp.stack([jnp.stack([jnp.where(rows >= sh, pows[sh - 1][0][None, :], 0.0),
                                  jnp.where(rows >= sh, pows[sh - 1][1][None, :], 0.0)]) for sh in (1, 2, 4)])
    pw_tab = jnp.stack([jnp.stack([p[0] for p in pows]), jnp.stack([p[1] for p in pows])])
    lb_tab = jnp.stack([jnp.broadcast_to(l1[0][None, :], (SUBLANES, N_STATE)),
                        jnp.broadcast_to(l1[1][None, :], (SUBLANES, N_STATE))])
    return (bb, cc, lb_tab, a_tab.astype(F32), pw_tab, d_skip.reshape(1, SSM_WIDTH),
            w_glu.astype(BF16), b_glu.reshape(1, SSM_WIDTH))


def _memkv_kernel(mem_ref, g_ref, wk_ref, wv_ref, kt_ref, vt_ref, ktb_ref, vtb_ref):
    mn = _rms(mem_ref[...], g_ref[...]).astype(BF16)
    kt = jnp.dot(mn, wk_ref[...], preferred_element_type=F32).T
    vt = jnp.dot(mn, wv_ref[...], preferred_element_type=F32).T
    kt_ref[0] = kt
    vt_ref[0] = vt
    ktb_ref[0] = kt.astype(BF16)
    vtb_ref[0] = vt.astype(BF16)


def _memory_kv(mem2d, g, wk, wv, n_batch):
    spec = pl.BlockSpec((1, CROSS_WIDTH, N_MEM), lambda n: (n, 0, 0))
    f = jax.ShapeDtypeStruct((n_batch, CROSS_WIDTH, N_MEM), F32)
    b = jax.ShapeDtypeStruct((n_batch, CROSS_WIDTH, N_MEM), BF16)
    return pl.pallas_call(
        _memkv_kernel,
        grid=(n_batch,),
        in_specs=[pl.BlockSpec((N_MEM, D_MODEL), lambda n: (n, 0)), _const_spec((1, D_MODEL)),
                  _const_spec((D_MODEL, CROSS_WIDTH)), _const_spec((D_MODEL, CROSS_WIDTH))],
        out_specs=[spec, spec, spec, spec],
        out_shape=[f, f, b, b],
        compiler_params=_params(("arbitrary",)),
        name="memory_kv",
    )(mem2d, g, wk, wv)


def _cross_attend(hq, kt, vt, n_q):
    heads = CROSS_WIDTH // CROSS_HEAD_DIM
    stacked = jnp.concatenate([hq] * heads, axis=0)
    r_h = lax.broadcasted_iota(jnp.int32, stacked.shape, 0) // n_q
    c_h = lax.broadcasted_iota(jnp.int32, stacked.shape, 1) // CROSS_HEAD_DIM
    own = r_h == c_h
    s = jnp.dot(jnp.where(own, stacked, 0.0).astype(BF16), kt, preferred_element_type=F32)
    p = jnp.exp(s - jnp.max(s, axis=-1, keepdims=True))
    p = p / jnp.sum(p, axis=-1, keepdims=True)
    full = lax.dot_general(p.astype(BF16), vt, (((1,), (1,)), ((), ())), preferred_element_type=F32)
    full = jnp.where(own, full, 0.0)
    out = full[0:n_q, :]
    for hh in range(1, heads):
        out = out + full[hh * n_q:(hh + 1) * n_q, :]
    return out


def _mix_cross_kernel(x_ref, o_ref, ys_ref, wo_ref, g_ref, wq_ref, kt_ref, vt_ref, wco_ref, out_ref,
                      *, per_seq, t_new):
    o = o_ref[...].astype(BF16)
    mix = (jnp.dot(o, wo_ref[0:ATTN_WIDTH, :], preferred_element_type=F32)
           + jnp.dot(ys_ref[...], wo_ref[ATTN_WIDTH:, :], preferred_element_type=F32))
    x1 = x_ref[...] + mix
    hq = jnp.dot(_rms(x1, g_ref[...]).astype(BF16), wq_ref[...], preferred_element_type=F32)
    hq = hq * (1.0 / math.sqrt(CROSS_HEAD_DIM))
    if per_seq:
        pieces = []
        for j in range(x1.shape[0] // t_new):
            pieces.append(_cross_attend(hq[j * t_new:(j + 1) * t_new, :], kt_ref[j].astype(BF16),
                                        vt_ref[j].astype(BF16), t_new))
        oc = jnp.concatenate(pieces, axis=0)
    else:
        oc = _cross_attend(hq, kt_ref[0], vt_ref[0], x1.shape[0])
    out_ref[...] = x1 + jnp.dot(oc.astype(BF16), wco_ref[...], preferred_element_type=F32)


def _mix_cross(x2d, o, ys, wo, g, wq, kt, vt, wco, per_seq, t_new, tiles_per_group, t):
    m = x2d.shape[0]
    if per_seq:
        nseq = t // t_new
        mem_spec = pl.BlockSpec((nseq, CROSS_WIDTH, N_MEM), lambda i: (i, 0, 0))
    else:
        mem_spec = pl.BlockSpec((1, CROSS_WIDTH, N_MEM), lambda i: (i // tiles_per_group, 0, 0))
    return pl.pallas_call(
        functools.partial(_mix_cross_kernel, per_seq=per_seq, t_new=t_new),
        grid=(m // t,),
        in_specs=[pl.BlockSpec((t, D_MODEL), lambda i: (i, 0)),
                  pl.BlockSpec((t, ATTN_WIDTH), lambda i: (i, 0)),
                  pl.BlockSpec((t, SSM_WIDTH), lambda i: (i, 0)),
                  _const_spec((D_MODEL, D_MODEL)), _const_spec((1, D_MODEL)),
                  _const_spec((D_MODEL, CROSS_WIDTH)), mem_spec, mem_spec,
                  _const_spec((CROSS_WIDTH, D_MODEL))],
        out_specs=pl.BlockSpec((t, D_MODEL), lambda i: (i, 0)),
        out_shape=jax.ShapeDtypeStruct((m, D_MODEL), F32),
        compiler_params=_params(("arbitrary",)),
        name="mix_cross",
    )(x2d, o, ys, wo, g, wq, kt, vt, wco)


def _mlp_kernel(x_ref, g_ref, wu_ref, wd_ref, gf_ref, y_ref, *, ff_chunk):
    x = x_ref[...]
    h = _rms(x, g_ref[...]).astype(BF16)
    acc = x
    for c in range(D_FF // ff_chunk):
        z = jnp.dot(h, wu_ref[:, c * ff_chunk:(c + 1) * ff_chunk], preferred_element_type=F32)
        a = jnp.square(jnp.maximum(z, 0.0)).astype(BF16)
        acc = acc + jnp.dot(a, wd_ref[c * ff_chunk:(c + 1) * ff_chunk, :], preferred_element_type=F32)
    y_ref[...] = _rms(acc, gf_ref[...])


def _mlp(x2d, g, wu, wd, gf, ff_chunk=1024):
    m = x2d.shape[0]
    t = TOK_TILE
    return pl.pallas_call(
        functools.partial(_mlp_kernel, ff_chunk=ff_chunk),
        grid=(m // t,),
        in_specs=[pl.BlockSpec((t, D_MODEL), lambda i: (i, 0)), _const_spec((1, D_MODEL)),
                  _const_spec((D_MODEL, D_FF)), _const_spec((D_FF, D_MODEL)), _const_spec((1, D_MODEL))],
        out_specs=pl.BlockSpec((t, D_MODEL), lambda i: (i, 0)),
        out_shape=jax.ShapeDtypeStruct((m, D_MODEL), F32),
        compiler_params=_params(("arbitrary",)),
        name="mlp",
    )(x2d, g, wu, wd, gf)


def kernel(x_prompt, x_sample, mem_prompt, cache_k, cache_v, page_table, state_ssm_re, state_ssm_im, cache_mem_k, cache_mem_v, norm_mix, w_in, lambda_q1, lambda_k1, lambda_q2, lambda_k2, subln_gain, ssm_a_re, ssm_a_im, ssm_log_dt, ssm_b_re, ssm_b_im, ssm_c_re, ssm_c_im, ssm_d, w_glu, b_glu, w_out, norm_cross, norm_mem, w_cq, w_ck, w_cv, w_co, norm_mlp, w_up, w_down, final_norm):
    n_p, t_p = x_prompt.shape[0], x_prompt.shape[1]
    n_s, t_s = x_sample.shape[0], x_sample.shape[1]
    n_pool = cache_k.shape[1]
    past = page_table.shape[1] * PAGE_SIZE
    assert cache_k.shape[0] == 1 and t_s == SUBLANES and n_p == SUBLANES
    assert t_p % TOK_TILE == 0 and (n_s * t_s) % TOK_TILE == 0

    l = 0
    w_in_b = w_in[l].astype(BF16)
    w_out_b = w_out[l].astype(BF16)
    w_cq_b, w_ck_b, w_cv_b, w_co_b = (w[l].astype(BF16) for w in (w_cq, w_ck, w_cv, w_co))
    w_up_b, w_down_b = w_up[l].astype(BF16), w_down[l].astype(BF16)
    lam_params = (lambda_q1, lambda_k1, lambda_q2, lambda_k2)
    ssm = _ssm_tables(ssm_a_re[l], ssm_a_im[l], ssm_log_dt[l], ssm_b_re[l], ssm_b_im[l],
                      ssm_c_re[l], ssm_c_im[l], ssm_d[l], w_glu[l], b_glu[l])
    final_g = final_norm.reshape(1, D_MODEL)

    xp = x_prompt.reshape(n_p * t_p, D_MODEL)
    tabs_p = _rope_tables(jnp.arange(t_p, dtype=jnp.int32))
    qt_p, kb_p, kt_p, vt_p, vlin_p, u_p = _project(xp, norm_mix, w_in_b, tabs_p, t_p // TOK_TILE, t_p, False)
    o_p = _prompt_attention(qt_p, kb_p, vt_p, lam_params, subln_gain.reshape(LANES, 1), n_p, t_p)
    ys_p, ht_p = _s5_prompt(u_p.reshape(n_p, t_p, SSM_WIDTH), ssm)
    mkt, mvt, mktb, mvtb = _memory_kv(mem_prompt.reshape(n_p * N_MEM, D_MODEL), norm_mem, w_ck_b, w_cv_b, n_p)
    x2_p = _mix_cross(xp, o_p, ys_p.reshape(n_p * t_p, SSM_WIDTH), w_out_b, norm_cross, w_cq_b, mktb, mvtb, w_co_b,
                      False, t_s, t_p // TOK_TILE, TOK_TILE)
    y_p = _mlp(x2_p, norm_mlp, w_up_b, w_down_b, final_g)

    xs = x_sample.reshape(n_s * t_s, D_MODEL)
    pos_s = past + (jnp.arange(TOK_TILE, dtype=jnp.int32) % t_s)
    tabs_s = _rope_tables(pos_s)
    q_s, k_s, v_s, vlin_s, u_s = _project(xs, norm_mix, w_in_b, tabs_s, 1, n_s * t_s, True)
    ck = jnp.transpose(cache_k, (0, 1, 3, 4, 5, 2)).reshape(n_pool, ATTN_WIDTH, PAGE_SIZE)
    cv = cache_v.reshape(n_pool, PAGE_SIZE * N_HEADS, LANES)
    o_s = _paged_attention(page_table.T, q_s, k_s, v_s, lam_params, subln_gain, ck, cv, n_s, t_s)
    h0_s = _state_pack(state_ssm_re[l].reshape(n_s, N_STATE), state_ssm_im[l].reshape(n_s, N_STATE))
    ys_s, ht_s = _s5_sample(u_s, h0_s, ssm)
    cmk = jnp.transpose(cache_mem_k[l], (0, 2, 3, 1)).reshape(n_s, CROSS_WIDTH, N_MEM)
    cmv = jnp.transpose(cache_mem_v[l], (0, 2, 3, 1)).reshape(n_s, CROSS_WIDTH, N_MEM)
    x2_s = _mix_cross(xs, o_s, ys_s, w_out_b, norm_cross, w_cq_b, cmk, cmv, w_co_b, True, t_s, 1, 128)
    y_s = _mlp(x2_s, norm_mlp, w_up_b, w_down_b, final_g)

    y_prompt = y_p.reshape(n_p, t_p, D_MODEL)
    y_sample = y_s.reshape(n_s, t_s, D_MODEL)
    k_prompt = jnp.transpose(kt_p.reshape(1, n_p, N_HEADS, 2, HEAD_DIM, t_p), (0, 1, 5, 2, 3, 4))
    v_prompt = vlin_p.reshape(1, n_p, t_p, N_HEADS, 2 * HEAD_DIM)
    state4 = lambda a, n: a.reshape(1, n, SSM_GROUPS, SSM_STATE)
    re_p, im_p = _state_unpack(ht_p)
    re_s, im_s = _state_unpack(ht_s)
    unpack_mem = lambda a: jnp.transpose(a.reshape(1, n_p, CROSS_WIDTH // CROSS_HEAD_DIM, CROSS_HEAD_DIM, N_MEM),
                                         (0, 1, 4, 2, 3))
    k_sample = k_s.reshape(1, n_s, t_s, N_HEADS, 2, HEAD_DIM)
    v_sample = vlin_s.reshape(1, n_s, t_s, N_HEADS, 2 * HEAD_DIM)
    return (y_prompt, y_sample, k_prompt, v_prompt, state4(re_p, n_p), state4(im_p, n_p),
            unpack_mem(mkt), unpack_mem(mvt), k_sample, v_sample, state4(re_s, n_s), state4(im_s, n_s))
```

```python
import functools
import math

import jax
import jax.numpy as jnp
from jax import lax
from jax.experimental import pallas as pl
from jax.experimental.pallas import tpu as pltpu

F32 = jnp.float32
BF16 = jnp.bfloat16

D_MODEL = 1024
HEAD_DIM = 64
N_HEADS = 4
ATTN_WIDTH = 512
ROT_DIM = 16
ROPE_THETA = 500000.0
SSM_WIDTH = 512
SSM_GROUP = 16
SSM_GROUPS = 32
SSM_STATE = 64
N_STATE = SSM_GROUPS * SSM_STATE
PAGE_SIZE = 128
N_MEM = 256
CROSS_WIDTH = 256
CROSS_HEAD_DIM = 64
D_FF = 4096
EPS = 1e-6
NEG_INF = -1e30
LAM_INIT = 0.8 - 0.6 * math.exp(-0.3 * 0)

LANES = 128
SUBLANES = 8
VMEM_LIMIT = 56 * 1024 * 1024

TOK_TILE = 256
PAGES_PER_CHUNK = 16


def _params(sem):
    return pltpu.CompilerParams(dimension_semantics=sem, vmem_limit_bytes=VMEM_LIMIT)


def _rms(x, g):
    ms = jnp.mean(x * x, axis=-1, keepdims=True)
    return x * lax.rsqrt(ms + EPS) * g


def _const_spec(shape):
    nd = len(shape)
    return pl.BlockSpec(shape, lambda *_: (0,) * nd)


def _rope(x, c, s1, s2):
    outs = []
    for i in range(x.shape[1] // LANES):
        xc = x[:, i * LANES:(i + 1) * LANES]
        outs.append(xc * c + pltpu.roll(xc, LANES - ROT_DIM // 2, 1) * s1 + pltpu.roll(xc, ROT_DIM // 2, 1) * s2)
    return jnp.concatenate(outs, axis=1)


def _proj_kernel(x_ref, g_ref, w_ref, c_ref, s1_ref, s2_ref, *out_refs, sample):
    h = _rms(x_ref[...], g_ref[...]).astype(BF16)
    proj = jnp.dot(h, w_ref[...], preferred_element_type=F32)
    c, s1, s2 = c_ref[...], s1_ref[...], s2_ref[...]
    q = _rope(proj[:, :ATTN_WIDTH], c, s1, s2) * (1.0 / math.sqrt(HEAD_DIM))
    k = _rope(proj[:, ATTN_WIDTH:2 * ATTN_WIDTH], c, s1, s2)
    v = proj[:, 2 * ATTN_WIDTH:3 * ATTN_WIDTH]
    if sample:
        q_ref, k_ref, v_ref, vlin_ref, u_ref = out_refs
        q_ref[...] = q
        k_ref[...] = k
        v_ref[...] = v
    else:
        qt_ref, kb_ref, kt_ref, vt_ref, vlin_ref, u_ref = out_refs
        qt_ref[0] = q.T.astype(BF16)
        kb_ref[...] = k.astype(BF16)
        kt_ref[0] = k.T
        vt_ref[0] = v.T.astype(BF16)
    for hh in range(N_HEADS):
        vlin_ref[pl.ds(hh, v.shape[0], stride=N_HEADS), :] = v[:, hh * LANES:(hh + 1) * LANES]
    u_ref[...] = proj[:, 3 * ATTN_WIDTH:]


def _project(x2d, g, w_bf, tabs, n_tab_tiles, rows_per_group, sample):
    m = x2d.shape[0]
    t = TOK_TILE
    groups = m // rows_per_group
    tiles_per_group = rows_per_group // t
    c, s1, s2 = tabs
    tab_spec = pl.BlockSpec((t, LANES), lambda i: (i % n_tab_tiles, 0))
    ft_spec = pl.BlockSpec((1, ATTN_WIDTH, t), lambda i: (i // tiles_per_group, 0, i % tiles_per_group))
    row_spec = pl.BlockSpec((t, ATTN_WIDTH), lambda i: (i, 0))
    lin_spec = pl.BlockSpec((t * N_HEADS, LANES), lambda i: (i, 0))
    row = lambda dt: jax.ShapeDtypeStruct((m, ATTN_WIDTH), dt)
    ft = lambda dt: jax.ShapeDtypeStruct((groups, ATTN_WIDTH, rows_per_group), dt)
    lin = jax.ShapeDtypeStruct((m * N_HEADS, LANES), F32)
    if sample:
        out_specs = [row_spec, row_spec, row_spec, lin_spec, row_spec]
        out_shape = [row(F32), row(F32), row(F32), lin, row(F32)]
    else:
        out_specs = [ft_spec, row_spec, ft_spec, ft_spec, lin_spec, row_spec]
        out_shape = [ft(BF16), row(BF16), ft(F32), ft(BF16), lin, row(F32)]
    return pl.pallas_call(
        functools.partial(_proj_kernel, sample=sample),
        grid=(m // t,),
        in_specs=[pl.BlockSpec((t, D_MODEL), lambda i: (i, 0)),
                  _const_spec((1, D_MODEL)),
                  _const_spec((D_MODEL, 4 * ATTN_WIDTH)),
                  tab_spec, tab_spec, tab_spec],
        out_specs=out_specs,
        out_shape=out_shape,
        compiler_params=_params(("arbitrary",)),
        name="proj",
    )(x2d, g, w_bf, c, s1, s2)


def _rope_tables(pos):
    half = ROT_DIM // 2
    inv_freq = jnp.float32(ROPE_THETA) ** (-jnp.arange(half, dtype=F32) * 2.0 / ROT_DIM)
    ang = pos.astype(F32)[:, None] * inv_freq[None, :]
    cos, sin = jnp.cos(ang), jnp.sin(ang)
    n = pos.shape[0]
    pad = jnp.zeros((n, HEAD_DIM - ROT_DIM), F32)
    c = jnp.concatenate([cos, cos, pad + 1.0], axis=1)
    s1 = jnp.concatenate([-sin, jnp.zeros_like(sin), pad], axis=1)
    s2 = jnp.concatenate([jnp.zeros_like(sin), sin, pad], axis=1)
    tile2 = lambda a: jnp.concatenate([a, a], axis=1)
    return tile2(c), tile2(s1), tile2(s2)


def _lam(lq1, lk1, lq2, lk2):
    return (jnp.exp(jnp.sum(lq1 * lk1, keepdims=True)) - jnp.exp(jnp.sum(lq2 * lk2, keepdims=True))
            + LAM_INIT)


def _subln(o, gain):
    ms = jnp.mean(o * o, axis=-1, keepdims=True)
    return o * lax.rsqrt(ms + EPS) * gain * (1.0 - LAM_INIT)


def _prompt_attn_kernel(qt_ref, k_ref, vt_ref, lq1, lk1, lq2, lk2, gain_ref, o_ref,
                        q2_s, m_s, l_s, acc_s, *, tq):
    qi = pl.program_id(2)
    qt = qt_ref[0]
    feat = lax.broadcasted_iota(jnp.int32, qt.shape, 0)
    zero = jnp.zeros_like(qt)
    q2_s[:, 0:tq] = jnp.where(feat < HEAD_DIM, qt, zero)
    q2_s[:, tq:2 * tq] = jnp.where(feat >= HEAD_DIM, qt, zero)
    m_s[...] = jnp.full(m_s.shape, NEG_INF, F32)
    l_s[...] = jnp.zeros(l_s.shape, F32)
    acc_s[...] = jnp.zeros(acc_s.shape, F32)

    def step(ki, masked):
        start = pl.multiple_of(ki * tq, tq)
        k = k_ref[pl.ds(start, tq), :]
        s = jnp.dot(k, q2_s[...], preferred_element_type=F32)
        if masked:
            key = lax.broadcasted_iota(jnp.int32, s.shape, 0)
            qry = lax.broadcasted_iota(jnp.int32, s.shape, 1) % tq
            s = jnp.where(key <= qry, s, NEG_INF)
        m_old = m_s[...]
        m_new = jnp.maximum(m_old, jnp.max(s, axis=0, keepdims=True))
        alpha = jnp.exp(m_old - m_new)
        p = jnp.exp(s - m_new)
        l_s[...] = alpha * l_s[...] + jnp.sum(p, axis=0, keepdims=True)
        vt = vt_ref[0, :, pl.ds(start, tq)]
        acc_s[...] = alpha * acc_s[...] + jnp.dot(vt, p.astype(BF16), preferred_element_type=F32)
        m_s[...] = m_new

    def body(ki, carry):
        step(ki, False)
        return carry

    lax.fori_loop(0, qi, body, 0)
    step(qi, True)

    lam = _lam(lq1[...], lk1[...], lq2[...], lk2[...])
    o = acc_s[...] / l_s[...]
    o = o[:, 0:tq] - lam * o[:, tq:2 * tq]
    ms = jnp.mean(o * o, axis=0, keepdims=True)
    o = o * lax.rsqrt(ms + EPS) * gain_ref[...] * (1.0 - LAM_INIT)
    o_ref[...] = o.T.astype(o_ref.dtype)


def _prompt_attention(qt_bf, k_bf, vt_bf, lam_params, gain_col, n_batch, seq, tq=512):
    nq = seq // tq
    lam_spec = _const_spec((1, HEAD_DIM))
    return pl.pallas_call(
        functools.partial(_prompt_attn_kernel, tq=tq),
        grid=(n_batch, N_HEADS, nq),
        in_specs=[pl.BlockSpec((1, LANES, tq), lambda n, h, i: (n, h, i)),
                  pl.BlockSpec((seq, LANES), lambda n, h, i: (n, h)),
                  pl.BlockSpec((1, LANES, seq), lambda n, h, i: (n, h, 0)),
                  lam_spec, lam_spec, lam_spec, lam_spec,
                  _const_spec((LANES, 1))],
        out_specs=pl.BlockSpec((tq, LANES), lambda n, h, i: (n * nq + i, h)),
        out_shape=jax.ShapeDtypeStruct((n_batch * seq, ATTN_WIDTH), BF16),
        scratch_shapes=[pltpu.VMEM((LANES, 2 * tq), BF16),
                        pltpu.VMEM((1, 2 * tq), F32),
                        pltpu.VMEM((1, 2 * tq), F32),
                        pltpu.VMEM((LANES, 2 * tq), F32)],
        compiler_params=_params(("arbitrary", "arbitrary", "arbitrary")),
        name="prompt_attn",
    )(qt_bf, k_bf, vt_bf, *lam_params, gain_col)


SEQS_PER_STEP = 2


def _paged_mlp_kernel(pt_ref, q_ref, knew_ref, vnew_ref, lq1, lk1, lq2, lk2, gain_ref,
                      x_ref, g_ref, gf_ref, ck_hbm, cv_hbm, wu_hbm, wd_hbm,
                      o_ref, y_ref, kbuf, vbuf, sem, wu_s, wd_s, wsem, h_s, acc_s,
                      *, n_steps, n_chunks, t_new):
    g_idx = pl.program_id(0)
    ppc = PAGES_PER_CHUNK
    spg = SEQS_PER_STEP
    rows = 2 * N_HEADS * t_new
    n_iter = spg * n_chunks
    ff_chunk = D_FF // n_iter

    def page_copies(seq, chunk, slot):
        cps = []
        for pg in range(ppc):
            page = pt_ref[chunk * ppc + pg, seq]
            cps.append(pltpu.make_async_copy(ck_hbm.at[page], kbuf.at[slot, pg], sem.at[0, slot]))
            cps.append(pltpu.make_async_copy(cv_hbm.at[page], vbuf.at[slot, pg], sem.at[1, slot]))
        return cps

    def weight_copies():
        return [pltpu.make_async_copy(wu_hbm, wu_s, wsem.at[0]), pltpu.make_async_copy(wd_hbm, wd_s, wsem.at[1])]

    @pl.when(g_idx == 0)
    def _():
        for cp in page_copies(0, 0, 0):
            cp.start()
        for cp in weight_copies():
            cp.start()
        for cp in weight_copies():
            cp.wait()

    def mlp_part(j):
        if j == 0:
            x = x_ref[...]
            h_s[...] = _rms(x, g_ref[...]).astype(BF16)
            acc_s[...] = x
        z = jnp.dot(h_s[...], wu_s[:, j * ff_chunk:(j + 1) * ff_chunk], preferred_element_type=F32)
        a = jnp.square(jnp.maximum(z, 0.0)).astype(BF16)
        acc_s[...] += jnp.dot(a, wd_s[j * ff_chunk:(j + 1) * ff_chunk, :], preferred_element_type=F32)
        if j == n_iter - 1:
            y_ref[...] = _rms(acc_s[...], gf_ref[...])

    lam = _lam(lq1[...], lk1[...], lq2[...], lk2[...])
    gain = gain_ref[...]

    for j in range(n_iter):
        i, c = divmod(j, n_chunks)
        seq = g_idx * spg + i
        slot = j % 2
        nslot = (j + 1) % 2
        if j + 1 < n_iter:
            i2, c2 = divmod(j + 1, n_chunks)
            for cp in page_copies(g_idx * spg + i2, c2, nslot):
                cp.start()
        else:
            @pl.when(g_idx + 1 < n_steps)
            def _():
                for cp in page_copies((g_idx + 1) * spg, 0, nslot):
                    cp.start()
        mlp_part(j)
        for cp in page_copies(seq, c, slot):
            cp.wait()
        if c == 0:
            q = q_ref[i * t_new:(i + 1) * t_new, :]
            qt = jnp.concatenate([q] * (2 * N_HEADS), axis=0)
            r_hj = lax.broadcasted_iota(jnp.int32, qt.shape, 0) // t_new
            c_hj = lax.broadcasted_iota(jnp.int32, qt.shape, 1) // HEAD_DIM
            qbd = jnp.where(r_hj == c_hj, qt, 0.0).astype(BF16)
            m = jnp.full((rows, 1), NEG_INF, F32)
            l = jnp.zeros((rows, 1), F32)
            acc = jnp.zeros((rows, ATTN_WIDTH), F32)
        kt = jnp.concatenate([kbuf[slot, pg].astype(BF16) for pg in range(ppc)], axis=1)
        s = jnp.dot(qbd, kt, preferred_element_type=F32)
        m_new = jnp.maximum(m, jnp.max(s, axis=-1, keepdims=True))
        alpha = jnp.exp(m - m_new)
        p = jnp.exp(s - m_new)
        l = alpha * l + jnp.sum(p, axis=-1, keepdims=True)
        v = jnp.concatenate(
            [jnp.concatenate([vbuf[slot, pg, pl.ds(hh, PAGE_SIZE, stride=N_HEADS), :] for hh in range(N_HEADS)],
                             axis=1) for pg in range(ppc)], axis=0).astype(BF16)
        acc = alpha * acc + jnp.dot(p.astype(BF16), v, preferred_element_type=F32)
        m = m_new
        if c < n_chunks - 1:
            continue

        pad = jnp.zeros((t_new, ATTN_WIDTH), F32)
        knew = jnp.concatenate([knew_ref[i * t_new:(i + 1) * t_new, :], pad], axis=0).astype(BF16)
        vnew = jnp.concatenate([vnew_ref[i * t_new:(i + 1) * t_new, :], pad], axis=0).astype(BF16)
        s = lax.dot_general(qbd, knew, (((1,), (1,)), ((), ())), preferred_element_type=F32)
        row_t = lax.broadcasted_iota(jnp.int32, s.shape, 0) % t_new
        col_t = lax.broadcasted_iota(jnp.int32, s.shape, 1)
        s = jnp.where(col_t <= row_t, s, NEG_INF)
        m_new = jnp.maximum(m, jnp.max(s, axis=-1, keepdims=True))
        alpha = jnp.exp(m - m_new)
        p = jnp.exp(s - m_new)
        l = alpha * l + jnp.sum(p, axis=-1, keepdims=True)
        acc = alpha * acc + jnp.dot(p.astype(BF16), vnew, preferred_element_type=F32)
        o_all = acc / l
        outs = []
        for hh in range(N_HEADS):
            r0 = hh * 2 * t_new
            blk = o_all[r0:r0 + 2 * t_new, hh * LANES:(hh + 1) * LANES]
            o = blk[0:t_new, :] - lam * blk[t_new:2 * t_new, :]
            outs.append(_subln(o, gain))
        o_ref[i * t_new:(i + 1) * t_new, :] = jnp.concatenate(outs, axis=1)


def _paged_mlp(pt_t, q, knew, vnew, lam_params, gain, ck, cv, x2d, g, gf, wu, wd, n_seq, t_new):
    n_pages = pt_t.shape[0]
    n_chunks = n_pages // PAGES_PER_CHUNK
    n_steps = n_seq // SEQS_PER_STEP
    m = x2d.shape[0]
    t = m // n_steps
    assert n_seq % SEQS_PER_STEP == 0 and m % n_steps == 0 and t % SUBLANES == 0
    assert D_FF % (SEQS_PER_STEP * n_chunks) == 0 and (SEQS_PER_STEP * n_chunks) % 2 == 0
    cmap = lambda s, pt: (0, 0)
    lam_spec = pl.BlockSpec((1, HEAD_DIM), cmap)
    row_spec = pl.BlockSpec((SEQS_PER_STEP * t_new, ATTN_WIDTH), lambda s, pt: (s, 0))
    x_spec = pl.BlockSpec((t, D_MODEL), lambda s, pt: (s, 0))
    vec_spec = pl.BlockSpec((1, D_MODEL), cmap)
    any_spec = pl.BlockSpec(memory_space=pl.ANY)
    feat = ck.shape[1]
    grid_spec = pltpu.PrefetchScalarGridSpec(
        num_scalar_prefetch=1,
        grid=(n_steps,),
        in_specs=[row_spec, row_spec, row_spec,
                  lam_spec, lam_spec, lam_spec, lam_spec,
                  pl.BlockSpec((1, LANES), cmap),
                  x_spec, vec_spec, vec_spec,
                  any_spec, any_spec, any_spec, any_spec],
        out_specs=[row_spec, x_spec],
        scratch_shapes=[pltpu.VMEM((2, PAGES_PER_CHUNK, feat, PAGE_SIZE), F32),
                        pltpu.VMEM((2, PAGES_PER_CHUNK, PAGE_SIZE * N_HEADS, LANES), F32),
                        pltpu.SemaphoreType.DMA((2, 2)),
                        pltpu.VMEM((D_MODEL, D_FF), BF16),
                        pltpu.VMEM((D_FF, D_MODEL), BF16),
                        pltpu.SemaphoreType.DMA((2,)),
                        pltpu.VMEM((t, D_MODEL), BF16),
                        pltpu.VMEM((t, D_MODEL), F32)],
    )
    return pl.pallas_call(
        functools.partial(_paged_mlp_kernel, n_steps=n_steps, n_chunks=n_chunks, t_new=t_new),
        grid_spec=grid_spec,
        out_shape=[jax.ShapeDtypeStruct((n_seq * t_new, ATTN_WIDTH), F32),
                   jax.ShapeDtypeStruct((m, D_MODEL), F32)],
        compiler_params=_params(("arbitrary",)),
        name="paged_mlp",
    )(pt_t, q, knew, vnew, *lam_params, gain, x2d, g, gf, ck, cv, wu, wd)


HALF_STATE = N_STATE // 2
S5_TIME_TILE = 32


def _s5_tail(x_bf, u, cc_ref, d_ref, wg_ref, bg_ref):
    y = jnp.concatenate(
        [jnp.dot(x_bf[:, hf * N_STATE:(hf + 1) * N_STATE], cc_ref[hf], preferred_element_type=F32) for hf in range(2)],
        axis=1) + d_ref[...] * u
    g = 0.5 * y * (1.0 + lax.erf(y * (1.0 / math.sqrt(2.0))))
    z = jnp.dot(g.astype(BF16), wg_ref[...], preferred_element_type=F32) + bg_ref[...]
    return g * (1.0 / (1.0 + jnp.exp(-z)))


def _s5_bu(u_bf, bb_ref, bu_s):
    half_u = SSM_WIDTH // 2
    for hf in range(2):
        bu_s[:, hf * N_STATE:(hf + 1) * N_STATE] = jnp.dot(u_bf[:, hf * half_u:(hf + 1) * half_u], bb_ref[hf],
                                                           preferred_element_type=F32)


def _s5_prompt_kernel(u_ref, perm_ref, permt_ref, bb_ref, cc_ref, lb_ref, d_ref, wg_ref, bg_ref,
                      ys_ref, ht_ref, bu_s, carry_s):
    nb, tt = u_ref.shape[0], u_ref.shape[1]
    rows = nb * tt

    @pl.when(pl.program_id(0) == 0)
    def _():
        carry_s[...] = jnp.zeros(carry_s.shape, F32)

    u = u_ref[...].reshape(rows, SSM_WIDTH)
    hi = u.astype(BF16)
    r1 = u - hi.astype(F32)
    mid = r1.astype(BF16)
    lo = (r1 - mid.astype(F32)).astype(BF16)
    perm = perm_ref[...]
    u_hi = jnp.dot(perm, hi, preferred_element_type=F32)
    u_tb = u_hi + jnp.dot(perm, mid, preferred_element_type=F32) + jnp.dot(perm, lo, preferred_element_type=F32)
    _s5_bu(u_hi.astype(BF16), bb_ref, bu_s)

    for hf in range(2):
        c0 = hf * N_STATE
        lbr = lb_ref[0, :, hf * HALF_STATE:(hf + 1) * HALF_STATE]
        lbi = lb_ref[1, :, hf * HALF_STATE:(hf + 1) * HALF_STATE]

        def body(t, carry, c0=c0, lbr=lbr, lbi=lbi):
            xr, xi = carry
            r0 = pl.multiple_of(t * nb, nb)
            br = bu_s[pl.ds(r0, nb), c0:c0 + HALF_STATE]
            bi = bu_s[pl.ds(r0, nb), c0 + HALF_STATE:c0 + N_STATE]
            xr, xi = lbr * xr - lbi * xi + br, lbr * xi + lbi * xr + bi
            bu_s[pl.ds(r0, nb), c0:c0 + HALF_STATE] = xr
            bu_s[pl.ds(r0, nb), c0 + HALF_STATE:c0 + N_STATE] = xi
            return xr, xi

        xr, xi = lax.fori_loop(0, tt, body, (carry_s[:, c0:c0 + HALF_STATE], carry_s[:, c0 + HALF_STATE:c0 + N_STATE]),
                               unroll=4)
        carry_s[:, c0:c0 + HALF_STATE] = xr
        carry_s[:, c0 + HALF_STATE:c0 + N_STATE] = xi
    ht_ref[...] = carry_s[...]

    ys_tb = _s5_tail(bu_s[...].astype(BF16), u_tb, cc_ref, d_ref, wg_ref, bg_ref).astype(BF16)
    ys = jnp.dot(permt_ref[...], ys_tb, preferred_element_type=F32)
    ys_ref[...] = ys.astype(ys_ref.dtype).reshape(nb, tt, SSM_WIDTH)


def _s5_prompt(u3, ssm):
    bb, cc, lb_tab, a_tab, pw_tab, d_row, wg, bg = ssm
    nb, seq = u3.shape[0], u3.shape[1]
    tt = S5_TIME_TILE
    rows = nb * tt
    r = jnp.arange(rows)
    perm = jax.nn.one_hot((r % nb) * tt + r // nb, rows, dtype=BF16)
    blk = pl.BlockSpec((nb, tt, SSM_WIDTH), lambda i: (0, i, 0))
    return pl.pallas_call(
        _s5_prompt_kernel,
        grid=(seq // tt,),
        in_specs=[blk, _const_spec((rows, rows)), _const_spec((rows, rows)),
                  _const_spec(bb.shape), _const_spec(cc.shape), _const_spec(lb_tab.shape),
                  _const_spec(d_row.shape), _const_spec(wg.shape), _const_spec(bg.shape)],
        out_specs=[blk, _const_spec((nb, 2 * N_STATE))],
        out_shape=[jax.ShapeDtypeStruct((nb, seq, SSM_WIDTH), BF16),
                   jax.ShapeDtypeStruct((nb, 2 * N_STATE), F32)],
        scratch_shapes=[pltpu.VMEM((rows, 2 * N_STATE), F32), pltpu.VMEM((nb, 2 * N_STATE), F32)],
        compiler_params=_params(("arbitrary",)),
        name="s5_prompt",
    )(u3, perm, perm.T, bb, cc, lb_tab, d_row, wg, bg)


def _s5_sample_kernel(u_ref, h0_ref, bb_ref, cc_ref, a_ref, pw_ref, d_ref, wg_ref, bg_ref,
                      ys_ref, ht_ref, bu_s):
    t = u_ref.shape[0]
    u = u_ref[...]
    _s5_bu(u.astype(BF16), bb_ref, bu_s)

    def cmul_add(xr, xi, ar, ai, sr, si):
        return xr + ar * sr - ai * si, xi + ar * si + ai * sr

    def body(b, carry):
        r0 = pl.multiple_of(b * SUBLANES, SUBLANES)
        init = h0_ref[pl.ds(b, 1), :]
        for hf in range(2):
            c0 = hf * N_STATE
            st = slice(hf * HALF_STATE, (hf + 1) * HALF_STATE)
            xr = bu_s[pl.ds(r0, SUBLANES), c0:c0 + HALF_STATE]
            xi = bu_s[pl.ds(r0, SUBLANES), c0 + HALF_STATE:c0 + N_STATE]
            for k, shift in enumerate((1, 2, 4)):
                sr = pltpu.roll(xr, shift, 0)
                si = pltpu.roll(xi, shift, 0)
                xr, xi = cmul_add(xr, xi, a_ref[k, 0, :, st], a_ref[k, 1, :, st], sr, si)
            cr = jnp.broadcast_to(init[:, c0:c0 + HALF_STATE], xr.shape)
            ci = jnp.broadcast_to(init[:, c0 + HALF_STATE:c0 + N_STATE], xi.shape)
            xr, xi = cmul_add(xr, xi, pw_ref[0, :, st], pw_ref[1, :, st], cr, ci)
            bu_s[pl.ds(r0, SUBLANES), c0:c0 + HALF_STATE] = xr
            bu_s[pl.ds(r0, SUBLANES), c0 + HALF_STATE:c0 + N_STATE] = xi
            ht_ref[pl.ds(b, 1), c0:c0 + HALF_STATE] = xr[SUBLANES - 1:SUBLANES, :]
            ht_ref[pl.ds(b, 1), c0 + HALF_STATE:c0 + N_STATE] = xi[SUBLANES - 1:SUBLANES, :]
        return carry

    lax.fori_loop(0, t // SUBLANES, body, 0)
    ys_ref[...] = _s5_tail(bu_s[...].astype(BF16), u, cc_ref, d_ref, wg_ref, bg_ref).astype(ys_ref.dtype)


def _s5_sample(u, h0, ssm):
    bb, cc, lb_tab, a_tab, pw_tab, d_row, wg, bg = ssm
    t = TOK_TILE
    m = u.shape[0]
    h_spec = pl.BlockSpec((t // SUBLANES, 2 * N_STATE), lambda i: (i, 0))
    return pl.pallas_call(
        _s5_sample_kernel,
        grid=(m // t,),
        in_specs=[pl.BlockSpec((t, SSM_WIDTH), lambda i: (i, 0)), h_spec,
                  _const_spec(bb.shape), _const_spec(cc.shape), _const_spec(a_tab.shape), _const_spec(pw_tab.shape),
                  _const_spec(d_row.shape), _const_spec(wg.shape), _const_spec(bg.shape)],
        out_specs=[pl.BlockSpec((t, SSM_WIDTH), lambda i: (i, 0)), h_spec],
        out_shape=[jax.ShapeDtypeStruct((m, SSM_WIDTH), BF16),
                   jax.ShapeDtypeStruct((m // SUBLANES, 2 * N_STATE), F32)],
        scratch_shapes=[pltpu.VMEM((t, 2 * N_STATE), F32)],
        compiler_params=_params(("arbitrary",)),
        name="s5_sample",
    )(u, h0, bb, cc, a_tab, pw_tab, d_row, wg, bg)


def _state_pack(re, im):
    return jnp.concatenate([re[:, :HALF_STATE], im[:, :HALF_STATE], re[:, HALF_STATE:], im[:, HALF_STATE:]], axis=1)


def _state_unpack(x):
    re = jnp.concatenate([x[:, 0:HALF_STATE], x[:, N_STATE:N_STATE + HALF_STATE]], axis=1)
    im = jnp.concatenate([x[:, HALF_STATE:N_STATE], x[:, N_STATE + HALF_STATE:]], axis=1)
    return re, im


def _ssm_tables(a_re, a_im, log_dt, b_re, b_im, c_re, c_im, d_skip, w_glu, b_glu):
    delta = jnp.exp(log_dt)[:, None]
    mag = jnp.exp(a_re * delta)
    ang = a_im * delta
    lb_re = mag * jnp.cos(ang)
    lb_im = mag * jnp.sin(ang)
    den = a_re * a_re + a_im * a_im
    num_re = lb_re - 1.0
    cz_re = (num_re * a_re + lb_im * a_im) / den
    cz_im = (lb_im * a_re - num_re * a_im) / den
    bb_re = cz_re[..., None] * b_re - cz_im[..., None] * b_im
    bb_im = cz_re[..., None] * b_im + cz_im[..., None] * b_re
    gh = SSM_GROUPS // 2
    eye = jnp.eye(gh, dtype=F32)
    bd_in = lambda w: jnp.einsum('gsp,gh->gphs', w, eye).reshape(gh * SSM_GROUP, HALF_STATE)
    bd_out = lambda w: jnp.einsum('gps,gh->gshp', w, eye).reshape(HALF_STATE, gh * SSM_GROUP)
    halves = lambda w: (w[:gh], w[gh:])
    bb = jnp.stack([jnp.concatenate([bd_in(r), bd_in(i)], axis=1)
                    for r, i in zip(halves(bb_re), halves(bb_im))]).astype(BF16)
    cc = jnp.stack([jnp.concatenate([bd_out(r), bd_out(-i)], axis=0)
                    for r, i in zip(halves(c_re), halves(c_im))]).astype(BF16)

    def cmul(ar, ai, br, bi):
        return ar * br - ai * bi, ar * bi + ai * br

    l1 = (lb_re.reshape(N_STATE), lb_im.reshape(N_STATE))
    pows = [l1]
    for _ in range(SUBLANES - 1):
        pows.append(cmul(*pows[-1], *l1))
    rows = jnp.arange(SUBLANES)[:, None]
    a_tab = jnp.stack([jnp.stack([jnp.where(rows >= sh, pows[sh - 1][0][None, :], 0.0),
                                  jnp.where(rows >= sh, pows[sh - 1][1][None, :], 0.0)]) for sh in (1, 2, 4)])
    pw_tab = jnp.stack([jnp.stack([p[0] for p in pows]), jnp.stack([p[1] for p in pows])])
    lb_tab = jnp.stack([jnp.broadcast_to(l1[0][None, :], (SUBLANES, N_STATE)),
                        jnp.broadcast_to(l1[1][None, :], (SUBLANES, N_STATE))])
    return (bb, cc, lb_tab, a_tab.astype(F32), pw_tab, d_skip.reshape(1, SSM_WIDTH),
            w_glu.astype(BF16), b_glu.reshape(1, SSM_WIDTH))


def _memkv_kernel(mem_ref, g_ref, wk_ref, wv_ref, kt_ref, vt_ref, ktb_ref, vtb_ref):
    mn = _rms(mem_ref[...], g_ref[...]).astype(BF16)
    kt = jnp.dot(mn, wk_ref[...], preferred_element_type=F32).T
    vt = jnp.dot(mn, wv_ref[...], preferred_element_type=F32).T
    kt_ref[0] = kt
    vt_ref[0] = vt
    ktb_ref[0] = kt.astype(BF16)
    vtb_ref[0] = vt.astype(BF16)


def _memory_kv(mem2d, g, wk, wv, n_batch):
    spec = pl.BlockSpec((1, CROSS_WIDTH, N_MEM), lambda n: (n, 0, 0))
    f = jax.ShapeDtypeStruct((n_batch, CROSS_WIDTH, N_MEM), F32)
    b = jax.ShapeDtypeStruct((n_batch, CROSS_WIDTH, N_MEM), BF16)
    return pl.pallas_call(
        _memkv_kernel,
        grid=(n_batch,),
        in_specs=[pl.BlockSpec((N_MEM, D_MODEL), lambda n: (n, 0)), _const_spec((1, D_MODEL)),
                  _const_spec((D_MODEL, CROSS_WIDTH)), _const_spec((D_MODEL, CROSS_WIDTH))],
        out_specs=[spec, spec, spec, spec],
        out_shape=[f, f, b, b],
        compiler_params=_params(("arbitrary",)),
        name="memory_kv",
    )(mem2d, g, wk, wv)


def _cross_attend(hq, kt, vt, n_q):
    heads = CROSS_WIDTH // CROSS_HEAD_DIM
    stacked = jnp.concatenate([hq] * heads, axis=0)
    r_h = lax.broadcasted_iota(jnp.int32, stacked.shape, 0) // n_q
    c_h = lax.broadcasted_iota(jnp.int32, stacked.shape, 1) // CROSS_HEAD_DIM
    own = r_h == c_h
    s = jnp.dot(jnp.where(own, stacked, 0.0).astype(BF16), kt, preferred_element_type=F32)
    p = jnp.exp(s - jnp.max(s, axis=-1, keepdims=True))
    p = p / jnp.sum(p, axis=-1, keepdims=True)
    full = lax.dot_general(p.astype(BF16), vt, (((1,), (1,)), ((), ())), preferred_element_type=F32)
    full = jnp.where(own, full, 0.0)
    out = full[0:n_q, :]
    for hh in range(1, heads):
        out = out + full[hh * n_q:(hh + 1) * n_q, :]
    return out


def _mix_cross_kernel(x_ref, o_ref, ys_ref, wo_ref, g_ref, wq_ref, kt_ref, vt_ref, wco_ref, out_ref,
                      *, per_seq, t_new):
    o = o_ref[...].astype(BF16)
    mix = (jnp.dot(o, wo_ref[0:ATTN_WIDTH, :], preferred_element_type=F32)
           + jnp.dot(ys_ref[...], wo_ref[ATTN_WIDTH:, :], preferred_element_type=F32))
    x1 = x_ref[...] + mix
    hq = jnp.dot(_rms(x1, g_ref[...]).astype(BF16), wq_ref[...], preferred_element_type=F32)
    hq = hq * (1.0 / math.sqrt(CROSS_HEAD_DIM))
    if per_seq:
        pieces = []
        for j in range(x1.shape[0] // t_new):
            pieces.append(_cross_attend(hq[j * t_new:(j + 1) * t_new, :], kt_ref[j].astype(BF16),
                                        vt_ref[j].astype(BF16), t_new))
        oc = jnp.concatenate(pieces, axis=0)
    else:
        oc = _cross_attend(hq, kt_ref[0], vt_ref[0], x1.shape[0])
    out_ref[...] = x1 + jnp.dot(oc.astype(BF16), wco_ref[...], preferred_element_type=F32)


def _mix_cross(x2d, o, ys, wo, g, wq, kt, vt, wco, per_seq, t_new, tiles_per_group, t):
    m = x2d.shape[0]
    if per_seq:
        nseq = t // t_new
        mem_spec = pl.BlockSpec((nseq, CROSS_WIDTH, N_MEM), lambda i: (i, 0, 0))
    else:
        mem_spec = pl.BlockSpec((1, CROSS_WIDTH, N_MEM), lambda i: (i // tiles_per_group, 0, 0))
    return pl.pallas_call(
        functools.partial(_mix_cross_kernel, per_seq=per_seq, t_new=t_new),
        grid=(m // t,),
        in_specs=[pl.BlockSpec((t, D_MODEL), lambda i: (i, 0)),
                  pl.BlockSpec((t, ATTN_WIDTH), lambda i: (i, 0)),
                  pl.BlockSpec((t, SSM_WIDTH), lambda i: (i, 0)),
                  _const_spec((D_MODEL, D_MODEL)), _const_spec((1, D_MODEL)),
                  _const_spec((D_MODEL, CROSS_WIDTH)), mem_spec, mem_spec,
                  _const_spec((CROSS_WIDTH, D_MODEL))],
        out_specs=pl.BlockSpec((t, D_MODEL), lambda i: (i, 0)),
        out_shape=jax.ShapeDtypeStruct((m, D_MODEL), F32),
        compiler_params=_params(("arbitrary",)),
        name="mix_cross",
    )(x2d, o, ys, wo, g, wq, kt, vt, wco)


def _mlp_kernel(x_ref, g_ref, wu_ref, wd_ref, gf_ref, y_ref, *, ff_chunk):
    x = x_ref[...]
    h = _rms(x, g_ref[...]).astype(BF16)
    acc = x
    for c in range(D_FF // ff_chunk):
        z = jnp.dot(h, wu_ref[:, c * ff_chunk:(c + 1) * ff_chunk], preferred_element_type=F32)
        a = jnp.square(jnp.maximum(z, 0.0)).astype(BF16)
        acc = acc + jnp.dot(a, wd_ref[c * ff_chunk:(c + 1) * ff_chunk, :], preferred_element_type=F32)
    y_ref[...] = _rms(acc, gf_ref[...])


def _mlp(x2d, g, wu, wd, gf, ff_chunk=1024):
    m = x2d.shape[0]
    t = TOK_TILE
    return pl.pallas_call(
        functools.partial(_mlp_kernel, ff_chunk=ff_chunk),
        grid=(m // t,),
        in_specs=[pl.BlockSpec((t, D_MODEL), lambda i: (i, 0)), _const_spec((1, D_MODEL)),
                  _const_spec((D_MODEL, D_FF)), _const_spec((D_FF, D_MODEL)), _const_spec((1, D_MODEL))],
        out_specs=pl.BlockSpec((t, D_MODEL), lambda i: (i, 0)),
        out_shape=jax.ShapeDtypeStruct((m, D_MODEL), F32),
        compiler_params=_params(("arbitrary",)),
        name="mlp",
    )(x2d, g, wu, wd, gf)


def kernel(x_prompt, x_sample, mem_prompt, cache_k, cache_v, page_table, state_ssm_re, state_ssm_im, cache_mem_k, cache_mem_v, norm_mix, w_in, lambda_q1, lambda_k1, lambda_q2, lambda_k2, subln_gain, ssm_a_re, ssm_a_im, ssm_log_dt, ssm_b_re, ssm_b_im, ssm_c_re, ssm_c_im, ssm_d, w_glu, b_glu, w_out, norm_cross, norm_mem, w_cq, w_ck, w_cv, w_co, norm_mlp, w_up, w_down, final_norm):
    n_p, t_p = x_prompt.shape[0], x_prompt.shape[1]
    n_s, t_s = x_sample.shape[0], x_sample.shape[1]
    n_pool = cache_k.shape[1]
    past = page_table.shape[1] * PAGE_SIZE
    assert cache_k.shape[0] == 1 and t_s == SUBLANES and n_p == SUBLANES
    assert t_p % TOK_TILE == 0 and (n_s * t_s) % TOK_TILE == 0

    l = 0
    w_in_b = w_in[l].astype(BF16)
    w_out_b = w_out[l].astype(BF16)
    w_cq_b, w_ck_b, w_cv_b, w_co_b = (w[l].astype(BF16) for w in (w_cq, w_ck, w_cv, w_co))
    w_up_b, w_down_b = w_up[l].astype(BF16), w_down[l].astype(BF16)
    lam_params = (lambda_q1, lambda_k1, lambda_q2, lambda_k2)
    ssm = _ssm_tables(ssm_a_re[l], ssm_a_im[l], ssm_log_dt[l], ssm_b_re[l], ssm_b_im[l],
                      ssm_c_re[l], ssm_c_im[l], ssm_d[l], w_glu[l], b_glu[l])
    final_g = final_norm.reshape(1, D_MODEL)

    xp = x_prompt.reshape(n_p * t_p, D_MODEL)
    tabs_p = _rope_tables(jnp.arange(t_p, dtype=jnp.int32))
    qt_p, kb_p, kt_p, vt_p, vlin_p, u_p = _project(xp, norm_mix, w_in_b, tabs_p, t_p // TOK_TILE, t_p, False)
    o_p = _prompt_attention(qt_p, kb_p, vt_p, lam_params, subln_gain.reshape(LANES, 1), n_p, t_p)
    ys_p, ht_p = _s5_prompt(u_p.reshape(n_p, t_p, SSM_WIDTH), ssm)
    mkt, mvt, mktb, mvtb = _memory_kv(mem_prompt.reshape(n_p * N_MEM, D_MODEL), norm_mem, w_ck_b, w_cv_b, n_p)
    x2_p = _mix_cross(xp, o_p, ys_p.reshape(n_p * t_p, SSM_WIDTH), w_out_b, norm_cross, w_cq_b, mktb, mvtb, w_co_b,
                      False, t_s, t_p // TOK_TILE, TOK_TILE)

    xs = x_sample.reshape(n_s * t_s, D_MODEL)
    pos_s = past + (jnp.arange(TOK_TILE, dtype=jnp.int32) % t_s)
    tabs_s = _rope_tables(pos_s)
    q_s, k_s, v_s, vlin_s, u_s = _project(xs, norm_mix, w_in_b, tabs_s, 1, n_s * t_s, True)
    ck = jnp.transpose(cache_k, (0, 1, 3, 4, 5, 2)).reshape(n_pool, ATTN_WIDTH, PAGE_SIZE)
    cv = cache_v.reshape(n_pool, PAGE_SIZE * N_HEADS, LANES)
    o_s, y_p = _paged_mlp(page_table.T, q_s, k_s, v_s, lam_params, subln_gain, ck, cv,
                          x2_p, norm_mlp, final_g, w_up_b, w_down_b, n_s, t_s)
    h0_s = _state_pack(state_ssm_re[l].reshape(n_s, N_STATE), state_ssm_im[l].reshape(n_s, N_STATE))
    ys_s, ht_s = _s5_sample(u_s, h0_s, ssm)
    cmk = jnp.transpose(cache_mem_k[l], (0, 2, 3, 1)).reshape(n_s, CROSS_WIDTH, N_MEM)
    cmv = jnp.transpose(cache_mem_v[l], (0, 2, 3, 1)).reshape(n_s, CROSS_WIDTH, N_MEM)
    x2_s = _mix_cross(xs, o_s, ys_s, w_out_b, norm_cross, w_cq_b, cmk, cmv, w_co_b, True, t_s, 1, 128)
    y_s = _mlp(x2_s, norm_mlp, w_up_b, w_down_b, final_g)

    y_prompt = y_p.reshape(n_p, t_p, D_MODEL)
    y_sample = y_s.reshape(n_s, t_s, D_MODEL)
    k_prompt = jnp.transpose(kt_p.reshape(1, n_p, N_HEADS, 2, HEAD_DIM, t_p), (0, 1, 5, 2, 3, 4))
    v_prompt = vlin_p.reshape(1, n_p, t_p, N_HEADS, 2 * HEAD_DIM)
    state4 = lambda a, n: a.reshape(1, n, SSM_GROUPS, SSM_STATE)
    re_p, im_p = _state_unpack(ht_p)
    re_s, im_s = _state_unpack(ht_s)
    unpack_mem = lambda a: jnp.transpose(a.reshape(1, n_p, CROSS_WIDTH // CROSS_HEAD_DIM, CROSS_HEAD_DIM, N_MEM),
                                         (0, 1, 4, 2, 3))
    k_sample = k_s.reshape(1, n_s, t_s, N_HEADS, 2, HEAD_DIM)
    v_sample = vlin_s.reshape(1, n_s, t_s, N_HEADS, 2 * HEAD_DIM)
    return (y_prompt, y_sample, k_prompt, v_prompt, state4(re_p, n_p), state4(im_p, n_p),
            unpack_mem(mkt), unpack_mem(mvt), k_sample, v_sample, state4(re_s, n_s), state4(im_s, n_s))
```

```python
import functools
import math

import jax
import jax.numpy as jnp
from jax import lax
from jax.experimental import pallas as pl
from jax.experimental.pallas import tpu as pltpu

F32 = jnp.float32
BF16 = jnp.bfloat16

D_MODEL = 1024
HEAD_DIM = 64
N_HEADS = 4
ATTN_WIDTH = 512
ROT_DIM = 16
ROPE_THETA = 500000.0
SSM_WIDTH = 512
SSM_GROUP = 16
SSM_GROUPS = 32
SSM_STATE = 64
N_STATE = SSM_GROUPS * SSM_STATE
PAGE_SIZE = 128
N_MEM = 256
CROSS_WIDTH = 256
CROSS_HEAD_DIM = 64
D_FF = 4096
EPS = 1e-6
NEG_INF = -1e30
LAM_INIT = 0.8 - 0.6 * math.exp(-0.3 * 0)

LANES = 128
SUBLANES = 8
VMEM_LIMIT = 56 * 1024 * 1024

TOK_TILE = 256
PAGES_PER_CHUNK = 16


def _params(sem):
    return pltpu.CompilerParams(dimension_semantics=sem, vmem_limit_bytes=VMEM_LIMIT)


def _rms(x, g):
    ms = jnp.mean(x * x, axis=-1, keepdims=True)
    return x * lax.rsqrt(ms + EPS) * g


def _const_spec(shape):
    nd = len(shape)
    return pl.BlockSpec(shape, lambda *_: (0,) * nd)


def _rope(x, c, s1, s2):
    outs = []
    for i in range(x.shape[1] // LANES):
        xc = x[:, i * LANES:(i + 1) * LANES]
        outs.append(xc * c + pltpu.roll(xc, LANES - ROT_DIM // 2, 1) * s1 + pltpu.roll(xc, ROT_DIM // 2, 1) * s2)
    return jnp.concatenate(outs, axis=1)


def _proj_kernel(x_ref, g_ref, w_ref, c_ref, s1_ref, s2_ref, *out_refs, sample):
    h = _rms(x_ref[...], g_ref[...]).astype(BF16)
    proj = jnp.dot(h, w_ref[...], preferred_element_type=F32)
    c, s1, s2 = c_ref[...], s1_ref[...], s2_ref[...]
    q = _rope(proj[:, :ATTN_WIDTH], c, s1, s2) * (1.0 / math.sqrt(HEAD_DIM))
    k = _rope(proj[:, ATTN_WIDTH:2 * ATTN_WIDTH], c, s1, s2)
    v = proj[:, 2 * ATTN_WIDTH:3 * ATTN_WIDTH]
    if sample:
        q_ref, k_ref, v_ref, vlin_ref, u_ref = out_refs
        q_ref[...] = q
        k_ref[...] = k
        v_ref[...] = v
    else:
        qt_ref, kb_ref, kt_ref, vt_ref, vlin_ref, u_ref = out_refs
        qt_ref[0] = q.T.astype(BF16)
        kb_ref[...] = k.astype(BF16)
        kt_ref[0] = k.T
        vt_ref[0] = v.T.astype(BF16)
    for hh in range(N_HEADS):
        vlin_ref[pl.ds(hh, v.shape[0], stride=N_HEADS), :] = v[:, hh * LANES:(hh + 1) * LANES]
    u_ref[...] = proj[:, 3 * ATTN_WIDTH:]


def _project(x2d, g, w_bf, tabs, n_tab_tiles, rows_per_group, sample):
    m = x2d.shape[0]
    t = TOK_TILE
    groups = m // rows_per_group
    tiles_per_group = rows_per_group // t
    c, s1, s2 = tabs
    tab_spec = pl.BlockSpec((t, LANES), lambda i: (i % n_tab_tiles, 0))
    ft_spec = pl.BlockSpec((1, ATTN_WIDTH, t), lambda i: (i // tiles_per_group, 0, i % tiles_per_group))
    row_spec = pl.BlockSpec((t, ATTN_WIDTH), lambda i: (i, 0))
    lin_spec = pl.BlockSpec((t * N_HEADS, LANES), lambda i: (i, 0))
    row = lambda dt: jax.ShapeDtypeStruct((m, ATTN_WIDTH), dt)
    ft = lambda dt: jax.ShapeDtypeStruct((groups, ATTN_WIDTH, rows_per_group), dt)
    lin = jax.ShapeDtypeStruct((m * N_HEADS, LANES), F32)
    if sample:
        out_specs = [row_spec, row_spec, row_spec, lin_spec, row_spec]
        out_shape = [row(F32), row(F32), row(F32), lin, row(F32)]
    else:
        out_specs = [ft_spec, row_spec, ft_spec, ft_spec, lin_spec, row_spec]
        out_shape = [ft(BF16), row(BF16), ft(F32), ft(BF16), lin, row(F32)]
    return pl.pallas_call(
        functools.partial(_proj_kernel, sample=sample),
        grid=(m // t,),
        in_specs=[pl.BlockSpec((t, D_MODEL), lambda i: (i, 0)),
                  _const_spec((1, D_MODEL)),
                  _const_spec((D_MODEL, 4 * ATTN_WIDTH)),
                  tab_spec, tab_spec, tab_spec],
        out_specs=out_specs,
        out_shape=out_shape,
        compiler_params=_params(("arbitrary",)),
        name="proj",
    )(x2d, g, w_bf, c, s1, s2)


def _rope_tables(pos):
    half = ROT_DIM // 2
    inv_freq = jnp.float32(ROPE_THETA) ** (-jnp.arange(half, dtype=F32) * 2.0 / ROT_DIM)
    ang = pos.astype(F32)[:, None] * inv_freq[None, :]
    cos, sin = jnp.cos(ang), jnp.sin(ang)
    n = pos.shape[0]
    pad = jnp.zeros((n, HEAD_DIM - ROT_DIM), F32)
    c = jnp.concatenate([cos, cos, pad + 1.0], axis=1)
    s1 = jnp.concatenate([-sin, jnp.zeros_like(sin), pad], axis=1)
    s2 = jnp.concatenate([jnp.zeros_like(sin), sin, pad], axis=1)
    tile2 = lambda a: jnp.concatenate([a, a], axis=1)
    return tile2(c), tile2(s1), tile2(s2)


def _lam(lq1, lk1, lq2, lk2):
    return (jnp.exp(jnp.sum(lq1 * lk1, keepdims=True)) - jnp.exp(jnp.sum(lq2 * lk2, keepdims=True))
            + LAM_INIT)


def _subln(o, gain):
    ms = jnp.mean(o * o, axis=-1, keepdims=True)
    return o * lax.rsqrt(ms + EPS) * gain * (1.0 - LAM_INIT)


def _prompt_attn_kernel(qt_ref, k_ref, vt_ref, lq1, lk1, lq2, lk2, gain_ref, o_ref,
                        q2_s, m_s, l_s, acc_s, *, tq):
    qi = pl.program_id(2)
    qt = qt_ref[0]
    feat = lax.broadcasted_iota(jnp.int32, qt.shape, 0)
    zero = jnp.zeros_like(qt)
    q2_s[:, 0:tq] = jnp.where(feat < HEAD_DIM, qt, zero)
    q2_s[:, tq:2 * tq] = jnp.where(feat >= HEAD_DIM, qt, zero)
    m_s[...] = jnp.full(m_s.shape, NEG_INF, F32)
    l_s[...] = jnp.zeros(l_s.shape, F32)
    acc_s[...] = jnp.zeros(acc_s.shape, F32)

    def step(ki, masked):
        start = pl.multiple_of(ki * tq, tq)
        k = k_ref[pl.ds(start, tq), :]
        vt = vt_ref[0, :, pl.ds(start, tq)]
        s = jnp.dot(k, q2_s[...], preferred_element_type=F32)
        if masked:
            key = lax.broadcasted_iota(jnp.int32, s.shape, 0)
            qry = lax.broadcasted_iota(jnp.int32, s.shape, 1) % tq
            s = jnp.where(key <= qry, s, NEG_INF)
        m_old = m_s[...]
        m_new = jnp.maximum(m_old, jnp.max(s, axis=0, keepdims=True))
        alpha = jnp.exp(m_old - m_new)
        p = jnp.exp(s - m_new)
        l_s[...] = alpha * l_s[...] + jnp.sum(p, axis=0, keepdims=True)
        acc_s[...] = alpha * acc_s[...] + jnp.dot(vt, p.astype(BF16), preferred_element_type=F32)
        m_s[...] = m_new

    def body(ki, carry):
        step(ki, False)
        return carry

    lax.fori_loop(0, qi, body, 0)
    step(qi, True)

    lam = _lam(lq1[...], lk1[...], lq2[...], lk2[...])
    o = acc_s[...] / l_s[...]
    o = o[:, 0:tq] - lam * o[:, tq:2 * tq]
    ms = jnp.mean(o * o, axis=0, keepdims=True)
    o = o * lax.rsqrt(ms + EPS) * gain_ref[...] * (1.0 - LAM_INIT)
    o_ref[...] = o.T.astype(o_ref.dtype)


def _prompt_attention(qt_bf, k_bf, vt_bf, lam_params, gain_col, n_batch, seq, tq=512):
    nq = seq // tq
    lam_spec = _const_spec((1, HEAD_DIM))
    return pl.pallas_call(
        functools.partial(_prompt_attn_kernel, tq=tq),
        grid=(n_batch, N_HEADS, nq),
        in_specs=[pl.BlockSpec((1, LANES, tq), lambda n, h, i: (n, h, i)),
                  pl.BlockSpec((seq, LANES), lambda n, h, i: (n, h)),
                  pl.BlockSpec((1, LANES, seq), lambda n, h, i: (n, h, 0)),
                  lam_spec, lam_spec, lam_spec, lam_spec,
                  _const_spec((LANES, 1))],
        out_specs=pl.BlockSpec((tq, LANES), lambda n, h, i: (n * nq + i, h)),
        out_shape=jax.ShapeDtypeStruct((n_batch * seq, ATTN_WIDTH), BF16),
        scratch_shapes=[pltpu.VMEM((LANES, 2 * tq), BF16),
                        pltpu.VMEM((1, 2 * tq), F32),
                        pltpu.VMEM((1, 2 * tq), F32),
                        pltpu.VMEM((LANES, 2 * tq), F32)],
        compiler_params=_params(("arbitrary", "arbitrary", "arbitrary")),
        name="prompt_attn",
    )(qt_bf, k_bf, vt_bf, *lam_params, gain_col)


SEQS_PER_STEP = 2
MLP_PARTS = 16


def _paged_mlp_kernel(pt_ref, q_ref, knew_ref, vnew_ref, lq1, lk1, lq2, lk2, gain_ref,
                      x_ref, po_ref, pys_ref, wo_ref, gc_ref, wq_ref, mkt_ref, mvt_ref, wco_ref, g_ref, gf_ref,
                      ck_hbm, cv_hbm, wu_hbm, wd_hbm,
                      o_ref, y_ref, kbuf, vbuf, sem, wu_s, wd_s, wsem, h_s, acc_s, hq_s,
                      *, n_steps, n_chunks, t_new):
    g_idx = pl.program_id(0)
    ppc = PAGES_PER_CHUNK
    spg = SEQS_PER_STEP
    rows = 2 * N_HEADS * t_new
    n_iter = spg * n_chunks
    n_mlp = MLP_PARTS
    ff_chunk = D_FF // n_mlp
    per_iter = -(-n_mlp // (n_iter - 2))
    schedule = [["mix"], ["cross"]] + [[] for _ in range(n_iter - 2)]
    for part in range(n_mlp):
        schedule[2 + part // per_iter].append(part)

    def page_copies(seq, chunk, slot):
        cps = []
        for pg in range(ppc):
            page = pt_ref[chunk * ppc + pg, seq]
            cps.append(pltpu.make_async_copy(ck_hbm.at[page], kbuf.at[slot, pg], sem.at[0, slot]))
            cps.append(pltpu.make_async_copy(cv_hbm.at[page], vbuf.at[slot, pg], sem.at[1, slot]))
        return cps

    def weight_copies():
        return [pltpu.make_async_copy(wu_hbm, wu_s, wsem.at[0]), pltpu.make_async_copy(wd_hbm, wd_s, wsem.at[1])]

    @pl.when(g_idx == 0)
    def _():
        for cp in page_copies(0, 0, 0):
            cp.start()
        for cp in weight_copies():
            cp.start()
        for cp in weight_copies():
            cp.wait()

    def dense_part(part):
        if part == "mix":
            mix = (jnp.dot(po_ref[...], wo_ref[0:ATTN_WIDTH, :], preferred_element_type=F32)
                   + jnp.dot(pys_ref[...], wo_ref[ATTN_WIDTH:, :], preferred_element_type=F32))
            x1 = x_ref[...] + mix
            acc_s[...] = x1
            hq = jnp.dot(_rms(x1, gc_ref[...]).astype(BF16), wq_ref[...], preferred_element_type=F32)
            hq_s[...] = hq * (1.0 / math.sqrt(CROSS_HEAD_DIM))
        elif part == "cross":
            oc = _cross_attend(hq_s[...], mkt_ref[0], mvt_ref[0], hq_s.shape[0])
            x2 = acc_s[...] + jnp.dot(oc.astype(BF16), wco_ref[...], preferred_element_type=F32)
            acc_s[...] = x2
            h_s[...] = _rms(x2, g_ref[...]).astype(BF16)
        else:
            cols = slice(part * ff_chunk, (part + 1) * ff_chunk)
            z = jnp.dot(h_s[...], wu_s[:, cols], preferred_element_type=F32)
            a = jnp.square(jnp.maximum(z, 0.0)).astype(BF16)
            acc_s[...] += jnp.dot(a, wd_s[cols, :], preferred_element_type=F32)
            if part == n_mlp - 1:
                y_ref[...] = _rms(acc_s[...], gf_ref[...])

    lam = _lam(lq1[...], lk1[...], lq2[...], lk2[...])
    gain = gain_ref[...]

    for j in range(n_iter):
        i, c = divmod(j, n_chunks)
        seq = g_idx * spg + i
        slot = j % 2
        nslot = (j + 1) % 2
        if j + 1 < n_iter:
            i2, c2 = divmod(j + 1, n_chunks)
            for cp in page_copies(g_idx * spg + i2, c2, nslot):
                cp.start()
        else:
            @pl.when(g_idx + 1 < n_steps)
            def _():
                for cp in page_copies((g_idx + 1) * spg, 0, nslot):
                    cp.start()
        for cp in page_copies(seq, c, slot):
            cp.wait()
        for part in schedule[j]:
            dense_part(part)
        if c == 0:
            q = q_ref[i * t_new:(i + 1) * t_new, :]
            qt = jnp.concatenate([q] * (2 * N_HEADS), axis=0)
            r_hj = lax.broadcasted_iota(jnp.int32, qt.shape, 0) // t_new
            c_hj = lax.broadcasted_iota(jnp.int32, qt.shape, 1) // HEAD_DIM
            qbd = jnp.where(r_hj == c_hj, qt, 0.0).astype(BF16)
            m = jnp.full((rows, 1), NEG_INF, F32)
            l = jnp.zeros((rows, 1), F32)
            acc = jnp.zeros((rows, ATTN_WIDTH), F32)
        kt = jnp.concatenate([kbuf[slot, pg].astype(BF16) for pg in range(ppc)], axis=1)
        s = jnp.dot(qbd, kt, preferred_element_type=F32)
        m_new = jnp.maximum(m, jnp.max(s, axis=-1, keepdims=True))
        alpha = jnp.exp(m - m_new)
        p = jnp.exp(s - m_new)
        l = alpha * l + jnp.sum(p, axis=-1, keepdims=True)
        v = jnp.concatenate(
            [jnp.concatenate([vbuf[slot, pg, pl.ds(hh, PAGE_SIZE, stride=N_HEADS), :] for hh in range(N_HEADS)],
                             axis=1) for pg in range(ppc)], axis=0).astype(BF16)
        acc = alpha * acc + jnp.dot(p.astype(BF16), v, preferred_element_type=F32)
        m = m_new
        if c < n_chunks - 1:
            continue

        pad = jnp.zeros((t_new, ATTN_WIDTH), F32)
        knew = jnp.concatenate([knew_ref[i * t_new:(i + 1) * t_new, :], pad], axis=0).astype(BF16)
        vnew = jnp.concatenate([vnew_ref[i * t_new:(i + 1) * t_new, :], pad], axis=0).astype(BF16)
        s = lax.dot_general(qbd, knew, (((1,), (1,)), ((), ())), preferred_element_type=F32)
        row_t = lax.broadcasted_iota(jnp.int32, s.shape, 0) % t_new
        col_t = lax.broadcasted_iota(jnp.int32, s.shape, 1)
        s = jnp.where(col_t <= row_t, s, NEG_INF)
        m_new = jnp.maximum(m, jnp.max(s, axis=-1, keepdims=True))
        alpha = jnp.exp(m - m_new)
        p = jnp.exp(s - m_new)
        l = alpha * l + jnp.sum(p, axis=-1, keepdims=True)
        acc = alpha * acc + jnp.dot(p.astype(BF16), vnew, preferred_element_type=F32)
        o_all = acc / l
        outs = []
        for hh in range(N_HEADS):
            r0 = hh * 2 * t_new
            blk = o_all[r0:r0 + 2 * t_new, hh * LANES:(hh + 1) * LANES]
            o = blk[0:t_new, :] - lam * blk[t_new:2 * t_new, :]
            outs.append(_subln(o, gain))
        o_ref[i * t_new:(i + 1) * t_new, :] = jnp.concatenate(outs, axis=1)


def _paged_mlp(pt_t, q, knew, vnew, lam_params, gain, ck, cv,
               x2d, po, pys, wo, gc, wq, mkt, mvt, wco, g, gf, wu, wd, n_seq, t_new, rows_per_mem):
    n_pages = pt_t.shape[0]
    n_chunks = n_pages // PAGES_PER_CHUNK
    n_steps = n_seq // SEQS_PER_STEP
    m = x2d.shape[0]
    t = m // n_steps
    n_iter = SEQS_PER_STEP * n_chunks
    assert n_seq % SEQS_PER_STEP == 0 and m % n_steps == 0 and rows_per_mem % t == 0
    assert n_iter % 2 == 0 and n_iter > 2 and D_FF % MLP_PARTS == 0
    cmap = lambda s, pt: (0, 0)
    lam_spec = pl.BlockSpec((1, HEAD_DIM), cmap)
    row_spec = pl.BlockSpec((SEQS_PER_STEP * t_new, ATTN_WIDTH), lambda s, pt: (s, 0))
    x_spec = pl.BlockSpec((t, D_MODEL), lambda s, pt: (s, 0))
    half_spec = pl.BlockSpec((t, ATTN_WIDTH), lambda s, pt: (s, 0))
    vec_spec = pl.BlockSpec((1, D_MODEL), cmap)
    mem_spec = pl.BlockSpec((1, CROSS_WIDTH, N_MEM), lambda s, pt: (s * t // rows_per_mem, 0, 0))
    any_spec = pl.BlockSpec(memory_space=pl.ANY)
    feat = ck.shape[1]
    grid_spec = pltpu.PrefetchScalarGridSpec(
        num_scalar_prefetch=1,
        grid=(n_steps,),
        in_specs=[row_spec, row_spec, row_spec,
                  lam_spec, lam_spec, lam_spec, lam_spec,
                  pl.BlockSpec((1, LANES), cmap),
                  x_spec, half_spec, half_spec,
                  pl.BlockSpec((D_MODEL, D_MODEL), cmap), vec_spec, pl.BlockSpec((D_MODEL, CROSS_WIDTH), cmap),
                  mem_spec, mem_spec, pl.BlockSpec((CROSS_WIDTH, D_MODEL), cmap),
                  vec_spec, vec_spec,
                  any_spec, any_spec, any_spec, any_spec],
        out_specs=[row_spec, x_spec],
        scratch_shapes=[pltpu.VMEM((2, PAGES_PER_CHUNK, feat, PAGE_SIZE), F32),
                        pltpu.VMEM((2, PAGES_PER_CHUNK, PAGE_SIZE * N_HEADS, LANES), F32),
                        pltpu.SemaphoreType.DMA((2, 2)),
                        pltpu.VMEM((D_MODEL, D_FF), BF16),
                        pltpu.VMEM((D_FF, D_MODEL), BF16),
                        pltpu.SemaphoreType.DMA((2,)),
                        pltpu.VMEM((t, D_MODEL), BF16),
                        pltpu.VMEM((t, D_MODEL), F32),
                        pltpu.VMEM((t, CROSS_WIDTH), F32)],
    )
    return pl.pallas_call(
        functools.partial(_paged_mlp_kernel, n_steps=n_steps, n_chunks=n_chunks, t_new=t_new),
        grid_spec=grid_spec,
        out_shape=[jax.ShapeDtypeStruct((n_seq * t_new, ATTN_WIDTH), F32),
                   jax.ShapeDtypeStruct((m, D_MODEL), F32)],
        compiler_params=_params(("arbitrary",)),
        name="paged_mlp",
    )(pt_t, q, knew, vnew, *lam_params, gain, x2d, po, pys, wo, gc, wq, mkt, mvt, wco, g, gf, ck, cv, wu, wd)


HALF_STATE = N_STATE // 2
S5_TIME_TILE = 32


def _s5_tail(x_bf, u, cc_ref, d_ref, wg_ref, bg_ref):
    y = jnp.concatenate(
        [jnp.dot(x_bf[:, hf * N_STATE:(hf + 1) * N_STATE], cc_ref[hf], preferred_element_type=F32) for hf in range(2)],
        axis=1) + d_ref[...] * u
    g = 0.5 * y * (1.0 + lax.erf(y * (1.0 / math.sqrt(2.0))))
    z = jnp.dot(g.astype(BF16), wg_ref[...], preferred_element_type=F32) + bg_ref[...]
    return g * (1.0 / (1.0 + jnp.exp(-z)))


def _s5_bu(u_bf, bb_ref, bu_s):
    half_u = SSM_WIDTH // 2
    for hf in range(2):
        bu_s[:, hf * N_STATE:(hf + 1) * N_STATE] = jnp.dot(u_bf[:, hf * half_u:(hf + 1) * half_u], bb_ref[hf],
                                                           preferred_element_type=F32)


def _s5_prompt_kernel(u_ref, perm_ref, permt_ref, bb_ref, cc_ref, lb_ref, d_ref, wg_ref, bg_ref,
                      ys_ref, ht_ref, bu_s, carry_s):
    nb, tt = u_ref.shape[0], u_ref.shape[1]
    rows = nb * tt

    @pl.when(pl.program_id(0) == 0)
    def _():
        carry_s[...] = jnp.zeros(carry_s.shape, F32)

    u = u_ref[...].reshape(rows, SSM_WIDTH)
    hi = u.astype(BF16)
    r1 = u - hi.astype(F32)
    mid = r1.astype(BF16)
    lo = (r1 - mid.astype(F32)).astype(BF16)
    perm = perm_ref[...]
    u_hi = jnp.dot(perm, hi, preferred_element_type=F32)
    u_tb = u_hi + jnp.dot(perm, mid, preferred_element_type=F32) + jnp.dot(perm, lo, preferred_element_type=F32)
    _s5_bu(u_hi.astype(BF16), bb_ref, bu_s)

    for hf in range(2):
        c0 = hf * N_STATE
        lbr = lb_ref[0, :, hf * HALF_STATE:(hf + 1) * HALF_STATE]
        lbi = lb_ref[1, :, hf * HALF_STATE:(hf + 1) * HALF_STATE]

        def body(t, carry, c0=c0, lbr=lbr, lbi=lbi):
            xr, xi = carry
            r0 = pl.multiple_of(t * nb, nb)
            br = bu_s[pl.ds(r0, nb), c0:c0 + HALF_STATE]
            bi = bu_s[pl.ds(r0, nb), c0 + HALF_STATE:c0 + N_STATE]
            xr, xi = lbr * xr - lbi * xi + br, lbr * xi + lbi * xr + bi
            bu_s[pl.ds(r0, nb), c0:c0 + HALF_STATE] = xr
            bu_s[pl.ds(r0, nb), c0 + HALF_STATE:c0 + N_STATE] = xi
            return xr, xi

        xr, xi = lax.fori_loop(0, tt, body, (carry_s[:, c0:c0 + HALF_STATE], carry_s[:, c0 + HALF_STATE:c0 + N_STATE]),
                               unroll=4)
        carry_s[:, c0:c0 + HALF_STATE] = xr
        carry_s[:, c0 + HALF_STATE:c0 + N_STATE] = xi
    ht_ref[...] = carry_s[...]

    ys_tb = _s5_tail(bu_s[...].astype(BF16), u_tb, cc_ref, d_ref, wg_ref, bg_ref).astype(BF16)
    ys = jnp.dot(permt_ref[...], ys_tb, preferred_element_type=F32)
    ys_ref[...] = ys.astype(ys_ref.dtype).reshape(nb, tt, SSM_WIDTH)


def _s5_prompt(u3, ssm):
    bb, cc, lb_tab, a_tab, pw_tab, d_row, wg, bg = ssm
    nb, seq = u3.shape[0], u3.shape[1]
    tt = S5_TIME_TILE
    rows = nb * tt
    r = jnp.arange(rows)
    perm = jax.nn.one_hot((r % nb) * tt + r // nb, rows, dtype=BF16)
    blk = pl.BlockSpec((nb, tt, SSM_WIDTH), lambda i: (0, i, 0))
    return pl.pallas_call(
        _s5_prompt_kernel,
        grid=(seq // tt,),
        in_specs=[blk, _const_spec((rows, rows)), _const_spec((rows, rows)),
                  _const_spec(bb.shape), _const_spec(cc.shape), _const_spec(lb_tab.shape),
                  _const_spec(d_row.shape), _const_spec(wg.shape), _const_spec(bg.shape)],
        out_specs=[blk, _const_spec((nb, 2 * N_STATE))],
        out_shape=[jax.ShapeDtypeStruct((nb, seq, SSM_WIDTH), BF16),
                   jax.ShapeDtypeStruct((nb, 2 * N_STATE), F32)],
        scratch_shapes=[pltpu.VMEM((rows, 2 * N_STATE), F32), pltpu.VMEM((nb, 2 * N_STATE), F32)],
        compiler_params=_params(("arbitrary",)),
        name="s5_prompt",
    )(u3, perm, perm.T, bb, cc, lb_tab, d_row, wg, bg)


def _s5_sample_kernel(u_ref, h0_ref, bb_ref, cc_ref, a_ref, pw_ref, d_ref, wg_ref, bg_ref,
                      ys_ref, ht_ref, bu_s):
    t = u_ref.shape[0]
    u = u_ref[...]
    _s5_bu(u.astype(BF16), bb_ref, bu_s)

    def cmul_add(xr, xi, ar, ai, sr, si):
        return xr + ar * sr - ai * si, xi + ar * si + ai * sr

    def body(b, carry):
        r0 = pl.multiple_of(b * SUBLANES, SUBLANES)
        init = h0_ref[pl.ds(b, 1), :]
        for hf in range(2):
            c0 = hf * N_STATE
            st = slice(hf * HALF_STATE, (hf + 1) * HALF_STATE)
            xr = bu_s[pl.ds(r0, SUBLANES), c0:c0 + HALF_STATE]
            xi = bu_s[pl.ds(r0, SUBLANES), c0 + HALF_STATE:c0 + N_STATE]
            for k, shift in enumerate((1, 2, 4)):
                sr = pltpu.roll(xr, shift, 0)
                si = pltpu.roll(xi, shift, 0)
                xr, xi = cmul_add(xr, xi, a_ref[k, 0, :, st], a_ref[k, 1, :, st], sr, si)
            cr = jnp.broadcast_to(init[:, c0:c0 + HALF_STATE], xr.shape)
            ci = jnp.broadcast_to(init[:, c0 + HALF_STATE:c0 + N_STATE], xi.shape)
            xr, xi = cmul_add(xr, xi, pw_ref[0, :, st], pw_ref[1, :, st], cr, ci)
            bu_s[pl.ds(r0, SUBLANES), c0:c0 + HALF_STATE] = xr
            bu_s[pl.ds(r0, SUBLANES), c0 + HALF_STATE:c0 + N_STATE] = xi
            ht_ref[pl.ds(b, 1), c0:c0 + HALF_STATE] = xr[SUBLANES - 1:SUBLANES, :]
            ht_ref[pl.ds(b, 1), c0 + HALF_STATE:c0 + N_STATE] = xi[SUBLANES - 1:SUBLANES, :]
        return carry

    lax.fori_loop(0, t // SUBLANES, body, 0)
    ys_ref[...] = _s5_tail(bu_s[...].astype(BF16), u, cc_ref, d_ref, wg_ref, bg_ref).astype(ys_ref.dtype)


def _s5_sample(u, h0, ssm):
    bb, cc, lb_tab, a_tab, pw_tab, d_row, wg, bg = ssm
    t = TOK_TILE
    m = u.shape[0]
    h_spec = pl.BlockSpec((t // SUBLANES, 2 * N_STATE), lambda i: (i, 0))
    return pl.pallas_call(
        _s5_sample_kernel,
        grid=(m // t,),
        in_specs=[pl.BlockSpec((t, SSM_WIDTH), lambda i: (i, 0)), h_spec,
                  _const_spec(bb.shape), _const_spec(cc.shape), _const_spec(a_tab.shape), _const_spec(pw_tab.shape),
                  _const_spec(d_row.shape), _const_spec(wg.shape), _const_spec(bg.shape)],
        out_specs=[pl.BlockSpec((t, SSM_WIDTH), lambda i: (i, 0)), h_spec],
        out_shape=[jax.ShapeDtypeStruct((m, SSM_WIDTH), BF16),
                   jax.ShapeDtypeStruct((m // SUBLANES, 2 * N_STATE), F32)],
        scratch_shapes=[pltpu.VMEM((t, 2 * N_STATE), F32)],
        compiler_params=_params(("arbitrary",)),
        name="s5_sample",
    )(u, h0, bb, cc, a_tab, pw_tab, d_row, wg, bg)


def _state_pack(re, im):
    return jnp.concatenate([re[:, :HALF_STATE], im[:, :HALF_STATE], re[:, HALF_STATE:], im[:, HALF_STATE:]], axis=1)


def _state_unpack(x):
    re = jnp.concatenate([x[:, 0:HALF_STATE], x[:, N_STATE:N_STATE + HALF_STATE]], axis=1)
    im = jnp.concatenate([x[:, HALF_STATE:N_STATE], x[:, N_STATE + HALF_STATE:]], axis=1)
    return re, im


def _ssm_tables(a_re, a_im, log_dt, b_re, b_im, c_re, c_im, d_skip, w_glu, b_glu):
    delta = jnp.exp(log_dt)[:, None]
    mag = jnp.exp(a_re * delta)
    ang = a_im * delta
    lb_re = mag * jnp.cos(ang)
    lb_im = mag * jnp.sin(ang)
    den = a_re * a_re + a_im * a_im
    num_re = lb_re - 1.0
    cz_re = (num_re * a_re + lb_im * a_im) / den
    cz_im = (lb_im * a_re - num_re * a_im) / den
    bb_re = cz_re[..., None] * b_re - cz_im[..., None] * b_im
    bb_im = cz_re[..., None] * b_im + cz_im[..., None] * b_re
    gh = SSM_GROUPS // 2
    eye = jnp.eye(gh, dtype=F32)
    bd_in = lambda w: jnp.einsum('gsp,gh->gphs', w, eye).reshape(gh * SSM_GROUP, HALF_STATE)
    bd_out = lambda w: jnp.einsum('gps,gh->gshp', w, eye).reshape(HALF_STATE, gh * SSM_GROUP)
    halves = lambda w: (w[:gh], w[gh:])
    bb = jnp.stack([jnp.concatenate([bd_in(r), bd_in(i)], axis=1)
                    for r, i in zip(halves(bb_re), halves(bb_im))]).astype(BF16)
    cc = jnp.stack([jnp.concatenate([bd_out(r), bd_out(-i)], axis=0)
                    for r, i in zip(halves(c_re), halves(c_im))]).astype(BF16)

    def cmul(ar, ai, br, bi):
        return ar * br - ai * bi, ar * bi + ai * br

    l1 = (lb_re.reshape(N_STATE), lb_im.reshape(N_STATE))
    pows = [l1]
    for _ in range(SUBLANES - 1):
        pows.append(cmul(*pows[-1], *l1))
    rows = jnp.arange(SUBLANES)[:, None]
    a_tab = jnp.stack([jnp.stack([jnp.where(rows >= sh, pows[sh - 1][0][None, :], 0.0),
                                  jnp.where(rows >= sh, pows[sh - 1][1][None, :], 0.0)]) for sh in (1, 2, 4)])
    pw_tab = jnp.stack([jnp.stack([p[0] for p in pows]), jnp.stack([p[1] for p in pows])])
    lb_tab = jnp.stack([jnp.broadcast_to(l1[0][None, :], (SUBLANES, N_STATE)),
                        jnp.broadcast_to(l1[1][None, :], (SUBLANES, N_STATE))])
    return (bb, cc, lb_tab, a_tab.astype(F32), pw_tab, d_skip.reshape(1, SSM_WIDTH),
            w_glu.astype(BF16), b_glu.reshape(1, SSM_WIDTH))


def _memkv_kernel(mem_ref, g_ref, wk_ref, wv_ref, kt_ref, vt_ref, ktb_ref, vtb_ref):
    mn = _rms(mem_ref[...], g_ref[...]).astype(BF16)
    kt = jnp.dot(mn, wk_ref[...], preferred_element_type=F32).T
    vt = jnp.dot(mn, wv_ref[...], preferred_element_type=F32).T
    kt_ref[0] = kt
    vt_ref[0] = vt
    ktb_ref[0] = kt.astype(BF16)
    vtb_ref[0] = vt.astype(BF16)


def _memory_kv(mem2d, g, wk, wv, n_batch):
    spec = pl.BlockSpec((1, CROSS_WIDTH, N_MEM), lambda n: (n, 0, 0))
    f = jax.ShapeDtypeStruct((n_batch, CROSS_WIDTH, N_MEM), F32)
    b = jax.ShapeDtypeStruct((n_batch, CROSS_WIDTH, N_MEM), BF16)
    return pl.pallas_call(
        _memkv_kernel,
        grid=(n_batch,),
        in_specs=[pl.BlockSpec((N_MEM, D_MODEL), lambda n: (n, 0)), _const_spec((1, D_MODEL)),
                  _const_spec((D_MODEL, CROSS_WIDTH)), _const_spec((D_MODEL, CROSS_WIDTH))],
        out_specs=[spec, spec, spec, spec],
        out_shape=[f, f, b, b],
        compiler_params=_params(("arbitrary",)),
        name="memory_kv",
    )(mem2d, g, wk, wv)


def _cross_attend(hq, kt, vt, n_q):
    heads = CROSS_WIDTH // CROSS_HEAD_DIM
    stacked = jnp.concatenate([hq] * heads, axis=0)
    r_h = lax.broadcasted_iota(jnp.int32, stacked.shape, 0) // n_q
    c_h = lax.broadcasted_iota(jnp.int32, stacked.shape, 1) // CROSS_HEAD_DIM
    own = r_h == c_h
    s = jnp.dot(jnp.where(own, stacked, 0.0).astype(BF16), kt, preferred_element_type=F32)
    p = jnp.exp(s - jnp.max(s, axis=-1, keepdims=True))
    p = p / jnp.sum(p, axis=-1, keepdims=True)
    full = lax.dot_general(p.astype(BF16), vt, (((1,), (1,)), ((), ())), preferred_element_type=F32)
    full = jnp.where(own, full, 0.0)
    out = full[0:n_q, :]
    for hh in range(1, heads):
        out = out + full[hh * n_q:(hh + 1) * n_q, :]
    return out


def _mix_cross_kernel(x_ref, o_ref, ys_ref, wo_ref, g_ref, wq_ref, kt_ref, vt_ref, wco_ref, out_ref, *, t_new):
    o = o_ref[...].astype(BF16)
    mix = (jnp.dot(o, wo_ref[0:ATTN_WIDTH, :], preferred_element_type=F32)
           + jnp.dot(ys_ref[...], wo_ref[ATTN_WIDTH:, :], preferred_element_type=F32))
    x1 = x_ref[...] + mix
    hq = jnp.dot(_rms(x1, g_ref[...]).astype(BF16), wq_ref[...], preferred_element_type=F32)
    hq = hq * (1.0 / math.sqrt(CROSS_HEAD_DIM))
    pieces = []
    for j in range(x1.shape[0] // t_new):
        pieces.append(_cross_attend(hq[j * t_new:(j + 1) * t_new, :], kt_ref[j].astype(BF16),
                                    vt_ref[j].astype(BF16), t_new))
    oc = jnp.concatenate(pieces, axis=0)
    out_ref[...] = x1 + jnp.dot(oc.astype(BF16), wco_ref[...], preferred_element_type=F32)


def _mix_cross(x2d, o, ys, wo, g, wq, kt, vt, wco, t_new, t):
    m = x2d.shape[0]
    mem_spec = pl.BlockSpec((t // t_new, CROSS_WIDTH, N_MEM), lambda i: (i, 0, 0))
    return pl.pallas_call(
        functools.partial(_mix_cross_kernel, t_new=t_new),
        grid=(m // t,),
        in_specs=[pl.BlockSpec((t, D_MODEL), lambda i: (i, 0)),
                  pl.BlockSpec((t, ATTN_WIDTH), lambda i: (i, 0)),
                  pl.BlockSpec((t, SSM_WIDTH), lambda i: (i, 0)),
                  _const_spec((D_MODEL, D_MODEL)), _const_spec((1, D_MODEL)),
                  _const_spec((D_MODEL, CROSS_WIDTH)), mem_spec, mem_spec,
                  _const_spec((CROSS_WIDTH, D_MODEL))],
        out_specs=pl.BlockSpec((t, D_MODEL), lambda i: (i, 0)),
        out_shape=jax.ShapeDtypeStruct((m, D_MODEL), F32),
        compiler_params=_params(("arbitrary",)),
        name="mix_cross",
    )(x2d, o, ys, wo, g, wq, kt, vt, wco)


def _mlp_kernel(x_ref, g_ref, wu_ref, wd_ref, gf_ref, y_ref, *, ff_chunk):
    x = x_ref[...]
    h = _rms(x, g_ref[...]).astype(BF16)
    acc = x
    for c in range(D_FF // ff_chunk):
        z = jnp.dot(h, wu_ref[:, c * ff_chunk:(c + 1) * ff_chunk], preferred_element_type=F32)
        a = jnp.square(jnp.maximum(z, 0.0)).astype(BF16)
        acc = acc + jnp.dot(a, wd_ref[c * ff_chunk:(c + 1) * ff_chunk, :], preferred_element_type=F32)
    y_ref[...] = _rms(acc, gf_ref[...])


def _mlp(x2d, g, wu, wd, gf, ff_chunk=1024):
    m = x2d.shape[0]
    t = TOK_TILE
    return pl.pallas_call(
        functools.partial(_mlp_kernel, ff_chunk=ff_chunk),
        grid=(m // t,),
        in_specs=[pl.BlockSpec((t, D_MODEL), lambda i: (i, 0)), _const_spec((1, D_MODEL)),
                  _const_spec((D_MODEL, D_FF)), _const_spec((D_FF, D_MODEL)), _const_spec((1, D_MODEL))],
        out_specs=pl.BlockSpec((t, D_MODEL), lambda i: (i, 0)),
        out_shape=jax.ShapeDtypeStruct((m, D_MODEL), F32),
        compiler_params=_params(("arbitrary",)),
        name="mlp",
    )(x2d, g, wu, wd, gf)


def kernel(x_prompt, x_sample, mem_prompt, cache_k, cache_v, page_table, state_ssm_re, state_ssm_im, cache_mem_k, cache_mem_v, norm_mix, w_in, lambda_q1, lambda_k1, lambda_q2, lambda_k2, subln_gain, ssm_a_re, ssm_a_im, ssm_log_dt, ssm_b_re, ssm_b_im, ssm_c_re, ssm_c_im, ssm_d, w_glu, b_glu, w_out, norm_cross, norm_mem, w_cq, w_ck, w_cv, w_co, norm_mlp, w_up, w_down, final_norm):
    n_p, t_p = x_prompt.shape[0], x_prompt.shape[1]
    n_s, t_s = x_sample.shape[0], x_sample.shape[1]
    n_pool = cache_k.shape[1]
    past = page_table.shape[1] * PAGE_SIZE
    assert cache_k.shape[0] == 1 and t_s == SUBLANES and n_p == SUBLANES
    assert t_p % TOK_TILE == 0 and (n_s * t_s) % TOK_TILE == 0

    l = 0
    w_in_b = w_in[l].astype(BF16)
    w_out_b = w_out[l].astype(BF16)
    w_cq_b, w_ck_b, w_cv_b, w_co_b = (w[l].astype(BF16) for w in (w_cq, w_ck, w_cv, w_co))
    w_up_b, w_down_b = w_up[l].astype(BF16), w_down[l].astype(BF16)
    lam_params = (lambda_q1, lambda_k1, lambda_q2, lambda_k2)
    ssm = _ssm_tables(ssm_a_re[l], ssm_a_im[l], ssm_log_dt[l], ssm_b_re[l], ssm_b_im[l],
                      ssm_c_re[l], ssm_c_im[l], ssm_d[l], w_glu[l], b_glu[l])
    final_g = final_norm.reshape(1, D_MODEL)

    xp = x_prompt.reshape(n_p * t_p, D_MODEL)
    tabs_p = _rope_tables(jnp.arange(t_p, dtype=jnp.int32))
    qt_p, kb_p, kt_p, vt_p, vlin_p, u_p = _project(xp, norm_mix, w_in_b, tabs_p, t_p // TOK_TILE, t_p, False)
    o_p = _prompt_attention(qt_p, kb_p, vt_p, lam_params, subln_gain.reshape(LANES, 1), n_p, t_p)
    ys_p, ht_p = _s5_prompt(u_p.reshape(n_p, t_p, SSM_WIDTH), ssm)
    mkt, mvt, mktb, mvtb = _memory_kv(mem_prompt.reshape(n_p * N_MEM, D_MODEL), norm_mem, w_ck_b, w_cv_b, n_p)

    xs = x_sample.reshape(n_s * t_s, D_MODEL)
    pos_s = past + (jnp.arange(TOK_TILE, dtype=jnp.int32) % t_s)
    tabs_s = _rope_tables(pos_s)
    q_s, k_s, v_s, vlin_s, u_s = _project(xs, norm_mix, w_in_b, tabs_s, 1, n_s * t_s, True)
    ck = jnp.transpose(cache_k, (0, 1, 3, 4, 5, 2)).reshape(n_pool, ATTN_WIDTH, PAGE_SIZE)
    cv = cache_v.reshape(n_pool, PAGE_SIZE * N_HEADS, LANES)
    o_s, y_p = _paged_mlp(page_table.T, q_s, k_s, v_s, lam_params, subln_gain, ck, cv,
                          xp, o_p, ys_p.reshape(n_p * t_p, SSM_WIDTH), w_out_b, norm_cross, w_cq_b, mktb, mvtb,
                          w_co_b, norm_mlp, final_g, w_up_b, w_down_b, n_s, t_s, t_p)
    h0_s = _state_pack(state_ssm_re[l].reshape(n_s, N_STATE), state_ssm_im[l].reshape(n_s, N_STATE))
    ys_s, ht_s = _s5_sample(u_s, h0_s, ssm)
    cmk = jnp.transpose(cache_mem_k[l], (0, 2, 3, 1)).reshape(n_s, CROSS_WIDTH, N_MEM)
    cmv = jnp.transpose(cache_mem_v[l], (0, 2, 3, 1)).reshape(n_s, CROSS_WIDTH, N_MEM)
    x2_s = _mix_cross(xs, o_s, ys_s, w_out_b, norm_cross, w_cq_b, cmk, cmv, w_co_b, t_s, 128)
    y_s = _mlp(x2_s, norm_mlp, w_up_b, w_down_b, final_g)

    y_prompt = y_p.reshape(n_p, t_p, D_MODEL)
    y_sample = y_s.reshape(n_s, t_s, D_MODEL)
    k_prompt = jnp.transpose(kt_p.reshape(1, n_p, N_HEADS, 2, HEAD_DIM, t_p), (0, 1, 5, 2, 3, 4))
    v_prompt = vlin_p.reshape(1, n_p, t_p, N_HEADS, 2 * HEAD_DIM)
    state4 = lambda a, n: a.reshape(1, n, SSM_GROUPS, SSM_STATE)
    re_p, im_p = _state_unpack(ht_p)
    re_s, im_s = _state_unpack(ht_s)
    unpack_mem = lambda a: jnp.transpose(a.reshape(1, n_p, CROSS_WIDTH // CROSS_HEAD_DIM, CROSS_HEAD_DIM, N_MEM),
                                         (0, 1, 4, 2, 3))
    k_sample = k_s.reshape(1, n_s, t_s, N_HEADS, 2, HEAD_DIM)
    v_sample = vlin_s.reshape(1, n_s, t_s, N_HEADS, 2 * HEAD_DIM)
    return (y_prompt, y_sample, k_prompt, v_prompt, state4(re_p, n_p), state4(im_p, n_p),
            unpack_mem(mkt), unpack_mem(mvt), k_sample, v_sample, state4(re_s, n_s), state4(im_s, n_s))
```

```python
import functools
import math

import jax
import jax.numpy as jnp
from jax import lax
from jax.experimental import pallas as pl
from jax.experimental.pallas import tpu as pltpu

F32 = jnp.float32
BF16 = jnp.bfloat16

D_MODEL = 1024
HEAD_DIM = 64
N_HEADS = 4
ATTN_WIDTH = 512
ROT_DIM = 16
ROPE_THETA = 500000.0
SSM_WIDTH = 512
SSM_GROUP = 16
SSM_GROUPS = 32
SSM_STATE = 64
N_STATE = SSM_GROUPS * SSM_STATE
PAGE_SIZE = 128
N_MEM = 256
CROSS_WIDTH = 256
CROSS_HEAD_DIM = 64
D_FF = 4096
EPS = 1e-6
NEG_INF = -1e30
LAM_INIT = 0.8 - 0.6 * math.exp(-0.3 * 0)

LANES = 128
SUBLANES = 8
VMEM_LIMIT = 56 * 1024 * 1024

TOK_TILE = 256
PAGES_PER_CHUNK = 16


def _params(sem):
    return pltpu.CompilerParams(dimension_semantics=sem, vmem_limit_bytes=VMEM_LIMIT)


def _rms(x, g):
    ms = jnp.mean(x * x, axis=-1, keepdims=True)
    return x * lax.rsqrt(ms + EPS) * g


def _const_spec(shape):
    nd = len(shape)
    return pl.BlockSpec(shape, lambda *_: (0,) * nd)


def _rope(x, c, s1, s2):
    outs = []
    for i in range(x.shape[1] // LANES):
        xc = x[:, i * LANES:(i + 1) * LANES]
        outs.append(xc * c + pltpu.roll(xc, LANES - ROT_DIM // 2, 1) * s1 + pltpu.roll(xc, ROT_DIM // 2, 1) * s2)
    return jnp.concatenate(outs, axis=1)


def _proj_kernel(x_ref, g_ref, w_ref, c_ref, s1_ref, s2_ref, *out_refs, sample):
    h = _rms(x_ref[...], g_ref[...]).astype(BF16)
    proj = jnp.dot(h, w_ref[...], preferred_element_type=F32)
    c, s1, s2 = c_ref[...], s1_ref[...], s2_ref[...]
    q = _rope(proj[:, :ATTN_WIDTH], c, s1, s2) * (1.0 / math.sqrt(HEAD_DIM))
    k = _rope(proj[:, ATTN_WIDTH:2 * ATTN_WIDTH], c, s1, s2)
    v = proj[:, 2 * ATTN_WIDTH:3 * ATTN_WIDTH]
    if sample:
        q_ref, k_ref, v_ref, vlin_ref, u_ref = out_refs
        q_ref[...] = q
        k_ref[...] = k
        v_ref[...] = v
    else:
        qt_ref, kb_ref, kt_ref, vt_ref, vlin_ref, u_ref = out_refs
        qt_ref[0] = q.T.astype(BF16)
        kb_ref[...] = k.astype(BF16)
        kt_ref[0] = k.T
        vt_ref[0] = v.T.astype(BF16)
    for hh in range(N_HEADS):
        vlin_ref[pl.ds(hh, v.shape[0], stride=N_HEADS), :] = v[:, hh * LANES:(hh + 1) * LANES]
    u_ref[...] = proj[:, 3 * ATTN_WIDTH:]


def _project(x2d, g, w_bf, tabs, n_tab_tiles, rows_per_group, sample):
    m = x2d.shape[0]
    t = TOK_TILE
    groups = m // rows_per_group
    tiles_per_group = rows_per_group // t
    c, s1, s2 = tabs
    tab_spec = pl.BlockSpec((t, LANES), lambda i: (i % n_tab_tiles, 0))
    ft_spec = pl.BlockSpec((1, ATTN_WIDTH, t), lambda i: (i // tiles_per_group, 0, i % tiles_per_group))
    row_spec = pl.BlockSpec((t, ATTN_WIDTH), lambda i: (i, 0))
    lin_spec = pl.BlockSpec((t * N_HEADS, LANES), lambda i: (i, 0))
    row = lambda dt: jax.ShapeDtypeStruct((m, ATTN_WIDTH), dt)
    ft = lambda dt: jax.ShapeDtypeStruct((groups, ATTN_WIDTH, rows_per_group), dt)
    lin = jax.ShapeDtypeStruct((m * N_HEADS, LANES), F32)
    if sample:
        out_specs = [row_spec, row_spec, row_spec, lin_spec, row_spec]
        out_shape = [row(F32), row(F32), row(F32), lin, row(F32)]
    else:
        out_specs = [ft_spec, row_spec, ft_spec, ft_spec, lin_spec, row_spec]
        out_shape = [ft(BF16), row(BF16), ft(F32), ft(BF16), lin, row(F32)]
    return pl.pallas_call(
        functools.partial(_proj_kernel, sample=sample),
        grid=(m // t,),
        in_specs=[pl.BlockSpec((t, D_MODEL), lambda i: (i, 0)),
                  _const_spec((1, D_MODEL)),
                  _const_spec((D_MODEL, 4 * ATTN_WIDTH)),
                  tab_spec, tab_spec, tab_spec],
        out_specs=out_specs,
        out_shape=out_shape,
        compiler_params=_params(("arbitrary",)),
        name="proj",
    )(x2d, g, w_bf, c, s1, s2)


def _rope_tables(pos):
    half = ROT_DIM // 2
    inv_freq = jnp.float32(ROPE_THETA) ** (-jnp.arange(half, dtype=F32) * 2.0 / ROT_DIM)
    ang = pos.astype(F32)[:, None] * inv_freq[None, :]
    cos, sin = jnp.cos(ang), jnp.sin(ang)
    n = pos.shape[0]
    pad = jnp.zeros((n, HEAD_DIM - ROT_DIM), F32)
    c = jnp.concatenate([cos, cos, pad + 1.0], axis=1)
    s1 = jnp.concatenate([-sin, jnp.zeros_like(sin), pad], axis=1)
    s2 = jnp.concatenate([jnp.zeros_like(sin), sin, pad], axis=1)
    tile2 = lambda a: jnp.concatenate([a, a], axis=1)
    return tile2(c), tile2(s1), tile2(s2)


def _lam(lq1, lk1, lq2, lk2):
    return (jnp.exp(jnp.sum(lq1 * lk1, keepdims=True)) - jnp.exp(jnp.sum(lq2 * lk2, keepdims=True))
            + LAM_INIT)


def _subln(o, gain):
    ms = jnp.mean(o * o, axis=-1, keepdims=True)
    return o * lax.rsqrt(ms + EPS) * gain * (1.0 - LAM_INIT)


ATTN_COLS = 256


def _prompt_attn_kernel(qt_ref, k_ref, vt_ref, lq1, lk1, lq2, lk2, gain_ref, o_ref,
                        q2_s, m_s, l_s, acc_s, *, tq):
    qi = pl.program_id(2)
    qt = qt_ref[0]
    feat = lax.broadcasted_iota(jnp.int32, qt.shape, 0)
    zero = jnp.zeros_like(qt)
    q2_s[:, 0:tq] = jnp.where(feat < HEAD_DIM, qt, zero)
    q2_s[:, tq:2 * tq] = jnp.where(feat >= HEAD_DIM, qt, zero)
    m_s[...] = jnp.full(m_s.shape, NEG_INF, F32)
    l_s[...] = jnp.zeros(l_s.shape, F32)
    acc_s[...] = jnp.zeros(acc_s.shape, F32)

    def step(ki, masked):
        start = pl.multiple_of(ki * tq, tq)
        k = k_ref[pl.ds(start, tq), :]
        vt = vt_ref[0, :, pl.ds(start, tq)]
        groups = [slice(g * ATTN_COLS, (g + 1) * ATTN_COLS) for g in range(2 * tq // ATTN_COLS)]
        scores = [jnp.dot(k, q2_s[:, cols], preferred_element_type=F32) for cols in groups]
        probs, stats = [], []
        for grp, (cols, s) in enumerate(zip(groups, scores)):
            if masked:
                key = lax.broadcasted_iota(jnp.int32, s.shape, 0)
                qry = (lax.broadcasted_iota(jnp.int32, s.shape, 1) + grp * ATTN_COLS) % tq
                s = jnp.where(key <= qry, s, NEG_INF)
            m_old = m_s[:, cols]
            m_new = jnp.maximum(m_old, jnp.max(s, axis=0, keepdims=True))
            alpha = jnp.exp(m_old - m_new)
            p = jnp.exp(s - m_new)
            probs.append(p.astype(BF16))
            stats.append((m_new, alpha, alpha * l_s[:, cols] + jnp.sum(p, axis=0, keepdims=True)))
        for cols, p, (m_new, alpha, l_new) in zip(groups, probs, stats):
            acc_s[:, cols] = alpha * acc_s[:, cols] + jnp.dot(vt, p, preferred_element_type=F32)
            m_s[:, cols] = m_new
            l_s[:, cols] = l_new

    def body(ki, carry):
        step(ki, False)
        return carry

    lax.fori_loop(0, qi, body, 0)
    step(qi, True)

    lam = _lam(lq1[...], lk1[...], lq2[...], lk2[...])
    o = acc_s[...] / l_s[...]
    o = o[:, 0:tq] - lam * o[:, tq:2 * tq]
    ms = jnp.mean(o * o, axis=0, keepdims=True)
    o = o * lax.rsqrt(ms + EPS) * gain_ref[...] * (1.0 - LAM_INIT)
    o_ref[...] = o.T.astype(o_ref.dtype)


def _prompt_attention(qt_bf, k_bf, vt_bf, lam_params, gain_col, n_batch, seq, tq=512):
    nq = seq // tq
    lam_spec = _const_spec((1, HEAD_DIM))
    return pl.pallas_call(
        functools.partial(_prompt_attn_kernel, tq=tq),
        grid=(n_batch, N_HEADS, nq),
        in_specs=[pl.BlockSpec((1, LANES, tq), lambda n, h, i: (n, h, i)),
                  pl.BlockSpec((seq, LANES), lambda n, h, i: (n, h)),
                  pl.BlockSpec((1, LANES, seq), lambda n, h, i: (n, h, 0)),
                  lam_spec, lam_spec, lam_spec, lam_spec,
                  _const_spec((LANES, 1))],
        out_specs=pl.BlockSpec((tq, LANES), lambda n, h, i: (n * nq + i, h)),
        out_shape=jax.ShapeDtypeStruct((n_batch * seq, ATTN_WIDTH), BF16),
        scratch_shapes=[pltpu.VMEM((LANES, 2 * tq), BF16),
                        pltpu.VMEM((1, 2 * tq), F32),
                        pltpu.VMEM((1, 2 * tq), F32),
                        pltpu.VMEM((LANES, 2 * tq), F32)],
        compiler_params=_params(("arbitrary", "arbitrary", "arbitrary")),
        name="prompt_attn",
    )(qt_bf, k_bf, vt_bf, *lam_params, gain_col)


SEQS_PER_STEP = 2
MLP_PARTS = 8


def _paged_mlp_kernel(pt_ref, q_ref, knew_ref, vnew_ref, lq1, lk1, lq2, lk2, gain_ref,
                      x_ref, po_ref, pys_ref, wo_ref, gc_ref, wq_ref, mkt_ref, mvt_ref, wco_ref, g_ref, gf_ref,
                      ck_hbm, cv_hbm, wu_hbm, wd_hbm,
                      o_ref, y_ref, kbuf, vbuf, sem, wu_s, wd_s, wsem, h_s, acc_s, hq_s,
                      *, n_steps, n_chunks, t_new):
    g_idx = pl.program_id(0)
    ppc = PAGES_PER_CHUNK
    spg = SEQS_PER_STEP
    rows = 2 * N_HEADS * t_new
    n_iter = spg * n_chunks
    n_mlp = MLP_PARTS
    ff_chunk = D_FF // n_mlp
    mlp_iters = n_iter - 2
    bounds = [round(i * n_mlp / mlp_iters) for i in range(mlp_iters + 1)]
    schedule = ["mix", "cross"] + [list(range(bounds[i], bounds[i + 1])) for i in range(mlp_iters)]

    def page_copies(seq, chunk, slot):
        cps = []
        for pg in range(ppc):
            page = pt_ref[chunk * ppc + pg, seq]
            cps.append(pltpu.make_async_copy(ck_hbm.at[page], kbuf.at[slot, pg], sem.at[0, slot]))
            cps.append(pltpu.make_async_copy(cv_hbm.at[page], vbuf.at[slot, pg], sem.at[1, slot]))
        return cps

    def wait_pages(slot):
        pltpu.make_async_copy(ck_hbm.at[pl.ds(0, ppc)], kbuf.at[slot], sem.at[0, slot]).wait()
        pltpu.make_async_copy(cv_hbm.at[pl.ds(0, ppc)], vbuf.at[slot], sem.at[1, slot]).wait()

    def weight_copies():
        return [pltpu.make_async_copy(wu_hbm, wu_s, wsem.at[0]), pltpu.make_async_copy(wd_hbm, wd_s, wsem.at[1])]

    @pl.when(g_idx == 0)
    def _():
        for cp in page_copies(0, 0, 0):
            cp.start()
        for cp in weight_copies():
            cp.start()
        for cp in weight_copies():
            cp.wait()

    def dense_part(part):
        if part == "mix":
            mix = (jnp.dot(po_ref[...], wo_ref[0:ATTN_WIDTH, :], preferred_element_type=F32)
                   + jnp.dot(pys_ref[...], wo_ref[ATTN_WIDTH:, :], preferred_element_type=F32))
            x1 = x_ref[...] + mix
            acc_s[...] = x1
            hq = jnp.dot(_rms(x1, gc_ref[...]).astype(BF16), wq_ref[...], preferred_element_type=F32)
            hq_s[...] = hq * (1.0 / math.sqrt(CROSS_HEAD_DIM))
        elif part == "cross":
            oc = _cross_attend(hq_s[...], mkt_ref[0], mvt_ref[0], hq_s.shape[0])
            x2 = acc_s[...] + jnp.dot(oc.astype(BF16), wco_ref[...], preferred_element_type=F32)
            acc_s[...] = x2
            h_s[...] = _rms(x2, g_ref[...]).astype(BF16)
        else:
            h = h_s[...]
            acc = acc_s[...]
            for sl in part:
                cols = slice(sl * ff_chunk, (sl + 1) * ff_chunk)
                z = jnp.dot(h, wu_s[:, cols], preferred_element_type=F32)
                a = jnp.square(jnp.maximum(z, 0.0)).astype(BF16)
                acc = acc + jnp.dot(a, wd_s[cols, :], preferred_element_type=F32)
            if part and part[-1] == n_mlp - 1:
                y_ref[...] = _rms(acc, gf_ref[...])
            else:
                acc_s[...] = acc

    lam = _lam(lq1[...], lk1[...], lq2[...], lk2[...])
    gain = gain_ref[...]

    for j in range(n_iter):
        i, c = divmod(j, n_chunks)
        seq = g_idx * spg + i
        slot = j % 2
        nslot = (j + 1) % 2
        if j + 1 < n_iter:
            i2, c2 = divmod(j + 1, n_chunks)
            for cp in page_copies(g_idx * spg + i2, c2, nslot):
                cp.start()
        else:
            @pl.when(g_idx + 1 < n_steps)
            def _():
                for cp in page_copies((g_idx + 1) * spg, 0, nslot):
                    cp.start()
        wait_pages(slot)
        dense_part(schedule[j])
        if c == 0:
            q = q_ref[i * t_new:(i + 1) * t_new, :]
            qt = jnp.concatenate([q] * (2 * N_HEADS), axis=0)
            r_hj = lax.broadcasted_iota(jnp.int32, qt.shape, 0) // t_new
            c_hj = lax.broadcasted_iota(jnp.int32, qt.shape, 1) // HEAD_DIM
            qbd = jnp.where(r_hj == c_hj, qt, 0.0).astype(BF16)
            m = jnp.full((rows, 1), NEG_INF, F32)
            l = jnp.zeros((rows, 1), F32)
            acc = jnp.zeros((rows, ATTN_WIDTH), F32)
        kt = jnp.concatenate([kbuf[slot, pg].astype(BF16) for pg in range(ppc)], axis=1)
        s = jnp.dot(qbd, kt, preferred_element_type=F32)
        m_new = jnp.maximum(m, jnp.max(s, axis=-1, keepdims=True))
        alpha = jnp.exp(m - m_new)
        p = jnp.exp(s - m_new)
        l = alpha * l + jnp.sum(p, axis=-1, keepdims=True)
        v = jnp.concatenate(
            [jnp.concatenate([vbuf[slot, pg, pl.ds(hh, PAGE_SIZE, stride=N_HEADS), :] for hh in range(N_HEADS)],
                             axis=1) for pg in range(ppc)], axis=0).astype(BF16)
        acc = alpha * acc + jnp.dot(p.astype(BF16), v, preferred_element_type=F32)
        m = m_new
        if c < n_chunks - 1:
            continue

        pad = jnp.zeros((t_new, ATTN_WIDTH), F32)
        knew = jnp.concatenate([knew_ref[i * t_new:(i + 1) * t_new, :], pad], axis=0).astype(BF16)
        vnew = jnp.concatenate([vnew_ref[i * t_new:(i + 1) * t_new, :], pad], axis=0).astype(BF16)
        s = lax.dot_general(qbd, knew, (((1,), (1,)), ((), ())), preferred_element_type=F32)
        row_t = lax.broadcasted_iota(jnp.int32, s.shape, 0) % t_new
        col_t = lax.broadcasted_iota(jnp.int32, s.shape, 1)
        s = jnp.where(col_t <= row_t, s, NEG_INF)
        m_new = jnp.maximum(m, jnp.max(s, axis=-1, keepdims=True))
        alpha = jnp.exp(m - m_new)
        p = jnp.exp(s - m_new)
        l = alpha * l + jnp.sum(p, axis=-1, keepdims=True)
        acc = alpha * acc + jnp.dot(p.astype(BF16), vnew, preferred_element_type=F32)
        o_all = acc / l
        outs = []
        for hh in range(N_HEADS):
            r0 = hh * 2 * t_new
            blk = o_all[r0:r0 + 2 * t_new, hh * LANES:(hh + 1) * LANES]
            o = blk[0:t_new, :] - lam * blk[t_new:2 * t_new, :]
            outs.append(_subln(o, gain))
        o_ref[i * t_new:(i + 1) * t_new, :] = jnp.concatenate(outs, axis=1)


def _paged_mlp(pt_t, q, knew, vnew, lam_params, gain, ck, cv,
               x2d, po, pys, wo, gc, wq, mkt, mvt, wco, g, gf, wu, wd, n_seq, t_new, rows_per_mem):
    n_pages = pt_t.shape[0]
    n_chunks = n_pages // PAGES_PER_CHUNK
    n_steps = n_seq // SEQS_PER_STEP
    m = x2d.shape[0]
    t = m // n_steps
    n_iter = SEQS_PER_STEP * n_chunks
    assert n_seq % SEQS_PER_STEP == 0 and m % n_steps == 0 and rows_per_mem % t == 0
    assert n_iter % 2 == 0 and n_iter > 2 and D_FF % MLP_PARTS == 0
    cmap = lambda s, pt: (0, 0)
    lam_spec = pl.BlockSpec((1, HEAD_DIM), cmap)
    row_spec = pl.BlockSpec((SEQS_PER_STEP * t_new, ATTN_WIDTH), lambda s, pt: (s, 0))
    x_spec = pl.BlockSpec((t, D_MODEL), lambda s, pt: (s, 0))
    half_spec = pl.BlockSpec((t, ATTN_WIDTH), lambda s, pt: (s, 0))
    vec_spec = pl.BlockSpec((1, D_MODEL), cmap)
    mem_spec = pl.BlockSpec((1, CROSS_WIDTH, N_MEM), lambda s, pt: (s * t // rows_per_mem, 0, 0))
    any_spec = pl.BlockSpec(memory_space=pl.ANY)
    feat = ck.shape[1]
    grid_spec = pltpu.PrefetchScalarGridSpec(
        num_scalar_prefetch=1,
        grid=(n_steps,),
        in_specs=[row_spec, row_spec, row_spec,
                  lam_spec, lam_spec, lam_spec, lam_spec,
                  pl.BlockSpec((1, LANES), cmap),
                  x_spec, half_spec, half_spec,
                  pl.BlockSpec((D_MODEL, D_MODEL), cmap), vec_spec, pl.BlockSpec((D_MODEL, CROSS_WIDTH), cmap),
                  mem_spec, mem_spec, pl.BlockSpec((CROSS_WIDTH, D_MODEL), cmap),
                  vec_spec, vec_spec,
                  any_spec, any_spec, any_spec, any_spec],
        out_specs=[row_spec, x_spec],
        scratch_shapes=[pltpu.VMEM((2, PAGES_PER_CHUNK, feat, PAGE_SIZE), F32),
                        pltpu.VMEM((2, PAGES_PER_CHUNK, PAGE_SIZE * N_HEADS, LANES), F32),
                        pltpu.SemaphoreType.DMA((2, 2)),
                        pltpu.VMEM((D_MODEL, D_FF), BF16),
                        pltpu.VMEM((D_FF, D_MODEL), BF16),
                        pltpu.SemaphoreType.DMA((2,)),
                        pltpu.VMEM((t, D_MODEL), BF16),
                        pltpu.VMEM((t, D_MODEL), F32),
                        pltpu.VMEM((t, CROSS_WIDTH), F32)],
    )
    return pl.pallas_call(
        functools.partial(_paged_mlp_kernel, n_steps=n_steps, n_chunks=n_chunks, t_new=t_new),
        grid_spec=grid_spec,
        out_shape=[jax.ShapeDtypeStruct((n_seq * t_new, ATTN_WIDTH), F32),
                   jax.ShapeDtypeStruct((m, D_MODEL), F32)],
        compiler_params=_params(("arbitrary",)),
        name="paged_mlp",
    )(pt_t, q, knew, vnew, *lam_params, gain, x2d, po, pys, wo, gc, wq, mkt, mvt, wco, g, gf, ck, cv, wu, wd)


HALF_STATE = N_STATE // 2
S5_TIME_TILE = 32


def _s5_tail(x_bf, u, cc_ref, d_ref, wg_ref, bg_ref):
    y = jnp.concatenate(
        [jnp.dot(x_bf[:, hf * N_STATE:(hf + 1) * N_STATE], cc_ref[hf], preferred_element_type=F32) for hf in range(2)],
        axis=1) + d_ref[...] * u
    g = 0.5 * y * (1.0 + lax.erf(y * (1.0 / math.sqrt(2.0))))
    z = jnp.dot(g.astype(BF16), wg_ref[...], preferred_element_type=F32) + bg_ref[...]
    return g * (1.0 / (1.0 + jnp.exp(-z)))


def _s5_bu(u_bf, bb_ref, bu_s):
    half_u = SSM_WIDTH // 2
    for hf in range(2):
        bu_s[:, hf * N_STATE:(hf + 1) * N_STATE] = jnp.dot(u_bf[:, hf * half_u:(hf + 1) * half_u], bb_ref[hf],
                                                           preferred_element_type=F32)


def _s5_prompt_kernel(u_ref, perm_ref, permt_ref, bb_ref, cc_ref, lb_ref, d_ref, wg_ref, bg_ref,
                      ys_ref, ht_ref, bu_s, carry_s):
    nb, tt = u_ref.shape[0], u_ref.shape[1]
    rows = nb * tt

    @pl.when(pl.program_id(0) == 0)
    def _():
        carry_s[...] = jnp.zeros(carry_s.shape, F32)

    u = u_ref[...].reshape(rows, SSM_WIDTH)
    hi = u.astype(BF16)
    r1 = u - hi.astype(F32)
    mid = r1.astype(BF16)
    lo = (r1 - mid.astype(F32)).astype(BF16)
    perm = perm_ref[...]
    u_hi = jnp.dot(perm, hi, preferred_element_type=F32)
    u_tb = u_hi + jnp.dot(perm, mid, preferred_element_type=F32) + jnp.dot(perm, lo, preferred_element_type=F32)
    _s5_bu(u_hi.astype(BF16), bb_ref, bu_s)

    for hf in range(2):
        c0 = hf * N_STATE
        lbr = lb_ref[0, :, hf * HALF_STATE:(hf + 1) * HALF_STATE]
        lbi = lb_ref[1, :, hf * HALF_STATE:(hf + 1) * HALF_STATE]

        def body(t, carry, c0=c0, lbr=lbr, lbi=lbi):
            xr, xi = carry
            r0 = pl.multiple_of(t * nb, nb)
            br = bu_s[pl.ds(r0, nb), c0:c0 + HALF_STATE]
            bi = bu_s[pl.ds(r0, nb), c0 + HALF_STATE:c0 + N_STATE]
            xr, xi = lbr * xr - lbi * xi + br, lbr * xi + lbi * xr + bi
            bu_s[pl.ds(r0, nb), c0:c0 + HALF_STATE] = xr
            bu_s[pl.ds(r0, nb), c0 + HALF_STATE:c0 + N_STATE] = xi
            return xr, xi

        xr, xi = lax.fori_loop(0, tt, body, (carry_s[:, c0:c0 + HALF_STATE], carry_s[:, c0 + HALF_STATE:c0 + N_STATE]),
                               unroll=4)
        carry_s[:, c0:c0 + HALF_STATE] = xr
        carry_s[:, c0 + HALF_STATE:c0 + N_STATE] = xi
    ht_ref[...] = carry_s[...]

    ys_tb = _s5_tail(bu_s[...].astype(BF16), u_tb, cc_ref, d_ref, wg_ref, bg_ref).astype(BF16)
    ys = jnp.dot(permt_ref[...], ys_tb, preferred_element_type=F32)
    ys_ref[...] = ys.astype(ys_ref.dtype).reshape(nb, tt, SSM_WIDTH)


def _s5_prompt(u3, ssm):
    bb, cc, lb_tab, a_tab, pw_tab, d_row, wg, bg = ssm
    nb, seq = u3.shape[0], u3.shape[1]
    tt = S5_TIME_TILE
    rows = nb * tt
    r = jnp.arange(rows)
    perm = jax.nn.one_hot((r % nb) * tt + r // nb, rows, dtype=BF16)
    blk = pl.BlockSpec((nb, tt, SSM_WIDTH), lambda i: (0, i, 0))
    return pl.pallas_call(
        _s5_prompt_kernel,
        grid=(seq // tt,),
        in_specs=[blk, _const_spec((rows, rows)), _const_spec((rows, rows)),
                  _const_spec(bb.shape), _const_spec(cc.shape), _const_spec(lb_tab.shape),
                  _const_spec(d_row.shape), _const_spec(wg.shape), _const_spec(bg.shape)],
        out_specs=[blk, _const_spec((nb, 2 * N_STATE))],
        out_shape=[jax.ShapeDtypeStruct((nb, seq, SSM_WIDTH), BF16),
                   jax.ShapeDtypeStruct((nb, 2 * N_STATE), F32)],
        scratch_shapes=[pltpu.VMEM((rows, 2 * N_STATE), F32), pltpu.VMEM((nb, 2 * N_STATE), F32)],
        compiler_params=_params(("arbitrary",)),
        name="s5_prompt",
    )(u3, perm, perm.T, bb, cc, lb_tab, d_row, wg, bg)


def _s5_sample_kernel(u_ref, h0_ref, bb_ref, cc_ref, a_ref, pw_ref, d_ref, wg_ref, bg_ref,
                      ys_ref, ht_ref, bu_s):
    t = u_ref.shape[0]
    u = u_ref[...]
    _s5_bu(u.astype(BF16), bb_ref, bu_s)

    def cmul_add(xr, xi, ar, ai, sr, si):
        return xr + ar * sr - ai * si, xi + ar * si + ai * sr

    def body(b, carry):
        r0 = pl.multiple_of(b * SUBLANES, SUBLANES)
        init = h0_ref[pl.ds(b, 1), :]
        for hf in range(2):
            c0 = hf * N_STATE
            st = slice(hf * HALF_STATE, (hf + 1) * HALF_STATE)
            xr = bu_s[pl.ds(r0, SUBLANES), c0:c0 + HALF_STATE]
            xi = bu_s[pl.ds(r0, SUBLANES), c0 + HALF_STATE:c0 + N_STATE]
            for k, shift in enumerate((1, 2, 4)):
                sr = pltpu.roll(xr, shift, 0)
                si = pltpu.roll(xi, shift, 0)
                xr, xi = cmul_add(xr, xi, a_ref[k, 0, :, st], a_ref[k, 1, :, st], sr, si)
            cr = jnp.broadcast_to(init[:, c0:c0 + HALF_STATE], xr.shape)
            ci = jnp.broadcast_to(init[:, c0 + HALF_STATE:c0 + N_STATE], xi.shape)
            xr, xi = cmul_add(xr, xi, pw_ref[0, :, st], pw_ref[1, :, st], cr, ci)
            bu_s[pl.ds(r0, SUBLANES), c0:c0 + HALF_STATE] = xr
            bu_s[pl.ds(r0, SUBLANES), c0 + HALF_STATE:c0 + N_STATE] = xi
            ht_ref[pl.ds(b, 1), c0:c0 + HALF_STATE] = xr[SUBLANES - 1:SUBLANES, :]
            ht_ref[pl.ds(b, 1), c0 + HALF_STATE:c0 + N_STATE] = xi[SUBLANES - 1:SUBLANES, :]
        return carry

    lax.fori_loop(0, t // SUBLANES, body, 0)
    ys_ref[...] = _s5_tail(bu_s[...].astype(BF16), u, cc_ref, d_ref, wg_ref, bg_ref).astype(ys_ref.dtype)


def _s5_sample(u, h0, ssm):
    bb, cc, lb_tab, a_tab, pw_tab, d_row, wg, bg = ssm
    t = TOK_TILE
    m = u.shape[0]
    h_spec = pl.BlockSpec((t // SUBLANES, 2 * N_STATE), lambda i: (i, 0))
    return pl.pallas_call(
        _s5_sample_kernel,
        grid=(m // t,),
        in_specs=[pl.BlockSpec((t, SSM_WIDTH), lambda i: (i, 0)), h_spec,
                  _const_spec(bb.shape), _const_spec(cc.shape), _const_spec(a_tab.shape), _const_spec(pw_tab.shape),
                  _const_spec(d_row.shape), _const_spec(wg.shape), _const_spec(bg.shape)],
        out_specs=[pl.BlockSpec((t, SSM_WIDTH), lambda i: (i, 0)), h_spec],
        out_shape=[jax.ShapeDtypeStruct((m, SSM_WIDTH), BF16),
                   jax.ShapeDtypeStruct((m // SUBLANES, 2 * N_STATE), F32)],
        scratch_shapes=[pltpu.VMEM((t, 2 * N_STATE), F32)],
        compiler_params=_params(("arbitrary",)),
        name="s5_sample",
    )(u, h0, bb, cc, a_tab, pw_tab, d_row, wg, bg)


def _state_pack(re, im):
    return jnp.concatenate([re[:, :HALF_STATE], im[:, :HALF_STATE], re[:, HALF_STATE:], im[:, HALF_STATE:]], axis=1)


def _state_unpack(x):
    re = jnp.concatenate([x[:, 0:HALF_STATE], x[:, N_STATE:N_STATE + HALF_STATE]], axis=1)
    im = jnp.concatenate([x[:, HALF_STATE:N_STATE], x[:, N_STATE + HALF_STATE:]], axis=1)
    return re, im


def _ssm_tables(a_re, a_im, log_dt, b_re, b_im, c_re, c_im, d_skip, w_glu, b_glu):
    delta = jnp.exp(log_dt)[:, None]
    mag = jnp.exp(a_re * delta)
    ang = a_im * delta
    lb_re = mag * jnp.cos(ang)
    lb_im = mag * jnp.sin(ang)
    den = a_re * a_re + a_im * a_im
    num_re = lb_re - 1.0
    cz_re = (num_re * a_re + lb_im * a_im) / den
    cz_im = (lb_im * a_re - num_re * a_im) / den
    bb_re = cz_re[..., None] * b_re - cz_im[..., None] * b_im
    bb_im = cz_re[..., None] * b_im + cz_im[..., None] * b_re
    gh = SSM_GROUPS // 2
    eye = jnp.eye(gh, dtype=F32)
    bd_in = lambda w: jnp.einsum('gsp,gh->gphs', w, eye).reshape(gh * SSM_GROUP, HALF_STATE)
    bd_out = lambda w: jnp.einsum('gps,gh->gshp', w, eye).reshape(HALF_STATE, gh * SSM_GROUP)
    halves = lambda w: (w[:gh], w[gh:])
    bb = jnp.stack([jnp.concatenate([bd_in(r), bd_in(i)], axis=1)
                    for r, i in zip(halves(bb_re), halves(bb_im))]).astype(BF16)
    cc = jnp.stack([jnp.concatenate([bd_out(r), bd_out(-i)], axis=0)
                    for r, i in zip(halves(c_re), halves(c_im))]).astype(BF16)

    def cmul(ar, ai, br, bi):
        return ar * br - ai * bi, ar * bi + ai * br

    l1 = (lb_re.reshape(N_STATE), lb_im.reshape(N_STATE))
    pows = [l1]
    for _ in range(SUBLANES - 1):
        pows.append(cmul(*pows[-1], *l1))
    rows = jnp.arange(SUBLANES)[:, None]
    a_tab = jnp.stack([jnp.stack([jnp.where(rows >= sh, pows[sh - 1][0][None, :], 0.0),
                                  jnp.where(rows >= sh, pows[sh - 1][1][None, :], 0.0)]) for sh in (1, 2, 4)])
    pw_tab = jnp.stack([jnp.stack([p[0] for p in pows]), jnp.stack([p[1] for p in pows])])
    lb_tab = jnp.stack([jnp.broadcast_to(l1[0][None, :], (SUBLANES, N_STATE)),
                        jnp.broadcast_to(l1[1][None, :], (SUBLANES, N_STATE))])
    return (bb, cc, lb_tab, a_tab.astype(F32), pw_tab, d_skip.reshape(1, SSM_WIDTH),
            w_glu.astype(BF16), b_glu.reshape(1, SSM_WIDTH))


def _memkv_kernel(mem_ref, g_ref, wk_ref, wv_ref, kt_ref, vt_ref, ktb_ref, vtb_ref):
    mn = _rms(mem_ref[...], g_ref[...]).astype(BF16)
    kt = jnp.dot(mn, wk_ref[...], preferred_element_type=F32).T
    vt = jnp.dot(mn, wv_ref[...], preferred_element_type=F32).T
    kt_ref[0] = kt
    vt_ref[0] = vt
    ktb_ref[0] = kt.astype(BF16)
    vtb_ref[0] = vt.astype(BF16)


def _memory_kv(mem2d, g, wk, wv, n_batch):
    spec = pl.BlockSpec((1, CROSS_WIDTH, N_MEM), lambda n: (n, 0, 0))
    f = jax.ShapeDtypeStruct((n_batch, CROSS_WIDTH, N_MEM), F32)
    b = jax.ShapeDtypeStruct((n_batch, CROSS_WIDTH, N_MEM), BF16)
    return pl.pallas_call(
        _memkv_kernel,
        grid=(n_batch,),
        in_specs=[pl.BlockSpec((N_MEM, D_MODEL), lambda n: (n, 0)), _const_spec((1, D_MODEL)),
                  _const_spec((D_MODEL, CROSS_WIDTH)), _const_spec((D_MODEL, CROSS_WIDTH))],
        out_specs=[spec, spec, spec, spec],
        out_shape=[f, f, b, b],
        compiler_params=_params(("arbitrary",)),
        name="memory_kv",
    )(mem2d, g, wk, wv)


def _cross_attend(hq, kt, vt, n_q):
    heads = CROSS_WIDTH // CROSS_HEAD_DIM
    stacked = jnp.concatenate([hq] * heads, axis=0)
    r_h = lax.broadcasted_iota(jnp.int32, stacked.shape, 0) // n_q
    c_h = lax.broadcasted_iota(jnp.int32, stacked.shape, 1) // CROSS_HEAD_DIM
    own = r_h == c_h
    s = jnp.dot(jnp.where(own, stacked, 0.0).astype(BF16), kt, preferred_element_type=F32)
    p = jnp.exp(s - jnp.max(s, axis=-1, keepdims=True))
    p = p / jnp.sum(p, axis=-1, keepdims=True)
    full = lax.dot_general(p.astype(BF16), vt, (((1,), (1,)), ((), ())), preferred_element_type=F32)
    full = jnp.where(own, full, 0.0)
    out = full[0:n_q, :]
    for hh in range(1, heads):
        out = out + full[hh * n_q:(hh + 1) * n_q, :]
    return out


def _mix_cross_kernel(x_ref, o_ref, ys_ref, wo_ref, g_ref, wq_ref, kt_ref, vt_ref, wco_ref, out_ref, *, t_new):
    o = o_ref[...].astype(BF16)
    mix = (jnp.dot(o, wo_ref[0:ATTN_WIDTH, :], preferred_element_type=F32)
           + jnp.dot(ys_ref[...], wo_ref[ATTN_WIDTH:, :], preferred_element_type=F32))
    x1 = x_ref[...] + mix
    hq = jnp.dot(_rms(x1, g_ref[...]).astype(BF16), wq_ref[...], preferred_element_type=F32)
    hq = hq * (1.0 / math.sqrt(CROSS_HEAD_DIM))
    pieces = []
    for j in range(x1.shape[0] // t_new):
        pieces.append(_cross_attend(hq[j * t_new:(j + 1) * t_new, :], kt_ref[j].astype(BF16),
                                    vt_ref[j].astype(BF16), t_new))
    oc = jnp.concatenate(pieces, axis=0)
    out_ref[...] = x1 + jnp.dot(oc.astype(BF16), wco_ref[...], preferred_element_type=F32)


def _mix_cross(x2d, o, ys, wo, g, wq, kt, vt, wco, t_new, t):
    m = x2d.shape[0]
    mem_spec = pl.BlockSpec((t // t_new, CROSS_WIDTH, N_MEM), lambda i: (i, 0, 0))
    return pl.pallas_call(
        functools.partial(_mix_cross_kernel, t_new=t_new),
        grid=(m // t,),
        in_specs=[pl.BlockSpec((t, D_MODEL), lambda i: (i, 0)),
                  pl.BlockSpec((t, ATTN_WIDTH), lambda i: (i, 0)),
                  pl.BlockSpec((t, SSM_WIDTH), lambda i: (i, 0)),
                  _const_spec((D_MODEL, D_MODEL)), _const_spec((1, D_MODEL)),
                  _const_spec((D_MODEL, CROSS_WIDTH)), mem_spec, mem_spec,
                  _const_spec((CROSS_WIDTH, D_MODEL))],
        out_specs=pl.BlockSpec((t, D_MODEL), lambda i: (i, 0)),
        out_shape=jax.ShapeDtypeStruct((m, D_MODEL), F32),
        compiler_params=_params(("arbitrary",)),
        name="mix_cross",
    )(x2d, o, ys, wo, g, wq, kt, vt, wco)


def _mlp_kernel(x_ref, g_ref, wu_ref, wd_ref, gf_ref, y_ref, *, ff_chunk):
    x = x_ref[...]
    h = _rms(x, g_ref[...]).astype(BF16)
    acc = x
    for c in range(D_FF // ff_chunk):
        z = jnp.dot(h, wu_ref[:, c * ff_chunk:(c + 1) * ff_chunk], preferred_element_type=F32)
        a = jnp.square(jnp.maximum(z, 0.0)).astype(BF16)
        acc = acc + jnp.dot(a, wd_ref[c * ff_chunk:(c + 1) * ff_chunk, :], preferred_element_type=F32)
    y_ref[...] = _rms(acc, gf_ref[...])


def _mlp(x2d, g, wu, wd, gf, ff_chunk=1024):
    m = x2d.shape[0]
    t = TOK_TILE
    return pl.pallas_call(
        functools.partial(_mlp_kernel, ff_chunk=ff_chunk),
        grid=(m // t,),
        in_specs=[pl.BlockSpec((t, D_MODEL), lambda i: (i, 0)), _const_spec((1, D_MODEL)),
                  _const_spec((D_MODEL, D_FF)), _const_spec((D_FF, D_MODEL)), _const_spec((1, D_MODEL))],
        out_specs=pl.BlockSpec((t, D_MODEL), lambda i: (i, 0)),
        out_shape=jax.ShapeDtypeStruct((m, D_MODEL), F32),
        compiler_params=_params(("arbitrary",)),
        name="mlp",
    )(x2d, g, wu, wd, gf)


def kernel(x_prompt, x_sample, mem_prompt, cache_k, cache_v, page_table, state_ssm_re, state_ssm_im, cache_mem_k, cache_mem_v, norm_mix, w_in, lambda_q1, lambda_k1, lambda_q2, lambda_k2, subln_gain, ssm_a_re, ssm_a_im, ssm_log_dt, ssm_b_re, ssm_b_im, ssm_c_re, ssm_c_im, ssm_d, w_glu, b_glu, w_out, norm_cross, norm_mem, w_cq, w_ck, w_cv, w_co, norm_mlp, w_up, w_down, final_norm):
    n_p, t_p = x_prompt.shape[0], x_prompt.shape[1]
    n_s, t_s = x_sample.shape[0], x_sample.shape[1]
    n_pool = cache_k.shape[1]
    past = page_table.shape[1] * PAGE_SIZE
    assert cache_k.shape[0] == 1 and t_s == SUBLANES and n_p == SUBLANES
    assert t_p % TOK_TILE == 0 and (n_s * t_s) % TOK_TILE == 0

    l = 0
    w_in_b = w_in[l].astype(BF16)
    w_out_b = w_out[l].astype(BF16)
    w_cq_b, w_ck_b, w_cv_b, w_co_b = (w[l].astype(BF16) for w in (w_cq, w_ck, w_cv, w_co))
    w_up_b, w_down_b = w_up[l].astype(BF16), w_down[l].astype(BF16)
    lam_params = (lambda_q1, lambda_k1, lambda_q2, lambda_k2)
    ssm = _ssm_tables(ssm_a_re[l], ssm_a_im[l], ssm_log_dt[l], ssm_b_re[l], ssm_b_im[l],
                      ssm_c_re[l], ssm_c_im[l], ssm_d[l], w_glu[l], b_glu[l])
    final_g = final_norm.reshape(1, D_MODEL)

    xp = x_prompt.reshape(n_p * t_p, D_MODEL)
    tabs_p = _rope_tables(jnp.arange(t_p, dtype=jnp.int32))
    qt_p, kb_p, kt_p, vt_p, vlin_p, u_p = _project(xp, norm_mix, w_in_b, tabs_p, t_p // TOK_TILE, t_p, False)
    o_p = _prompt_attention(qt_p, kb_p, vt_p, lam_params, subln_gain.reshape(LANES, 1), n_p, t_p)
    ys_p, ht_p = _s5_prompt(u_p.reshape(n_p, t_p, SSM_WIDTH), ssm)
    mkt, mvt, mktb, mvtb = _memory_kv(mem_prompt.reshape(n_p * N_MEM, D_MODEL), norm_mem, w_ck_b, w_cv_b, n_p)

    xs = x_sample.reshape(n_s * t_s, D_MODEL)
    pos_s = past + (jnp.arange(TOK_TILE, dtype=jnp.int32) % t_s)
    tabs_s = _rope_tables(pos_s)
    q_s, k_s, v_s, vlin_s, u_s = _project(xs, norm_mix, w_in_b, tabs_s, 1, n_s * t_s, True)
    ck = jnp.transpose(cache_k, (0, 1, 3, 4, 5, 2)).reshape(n_pool, ATTN_WIDTH, PAGE_SIZE)
    cv = cache_v.reshape(n_pool, PAGE_SIZE * N_HEADS, LANES)
    o_s, y_p = _paged_mlp(page_table.T, q_s, k_s, v_s, lam_params, subln_gain, ck, cv,
                          xp, o_p, ys_p.reshape(n_p * t_p, SSM_WIDTH), w_out_b, norm_cross, w_cq_b, mktb, mvtb,
                          w_co_b, norm_mlp, final_g, w_up_b, w_down_b, n_s, t_s, t_p)
    h0_s = _state_pack(state_ssm_re[l].reshape(n_s, N_STATE), state_ssm_im[l].reshape(n_s, N_STATE))
    ys_s, ht_s = _s5_sample(u_s, h0_s, ssm)
    cmk = jnp.transpose(cache_mem_k[l], (0, 2, 3, 1)).reshape(n_s, CROSS_WIDTH, N_MEM)
    cmv = jnp.transpose(cache_mem_v[l], (0, 2, 3, 1)).reshape(n_s, CROSS_WIDTH, N_MEM)
    x2_s = _mix_cross(xs, o_s, ys_s, w_out_b, norm_cross, w_cq_b, cmk, cmv, w_co_b, t_s, 128)
    y_s = _mlp(x2_s, norm_mlp, w_up_b, w_down_b, final_g)

    y_prompt = y_p.reshape(n_p, t_p, D_MODEL)
    y_sample = y_s.reshape(n_s, t_s, D_MODEL)
    k_prompt = jnp.transpose(kt_p.reshape(1, n_p, N_HEADS, 2, HEAD_DIM, t_p), (0, 1, 5, 2, 3, 4))
    v_prompt = vlin_p.reshape(1, n_p, t_p, N_HEADS, 2 * HEAD_DIM)
    state4 = lambda a, n: a.reshape(1, n, SSM_GROUPS, SSM_STATE)
    re_p, im_p = _state_unpack(ht_p)
    re_s, im_s = _state_unpack(ht_s)
    unpack_mem = lambda a: jnp.transpose(a.reshape(1, n_p, CROSS_WIDTH // CROSS_HEAD_DIM, CROSS_HEAD_DIM, N_MEM),
                                         (0, 1, 4, 2, 3))
    k_sample = k_s.reshape(1, n_s, t_s, N_HEADS, 2, HEAD_DIM)
    v_sample = vlin_s.reshape(1, n_s, t_s, N_HEADS, 2 * HEAD_DIM)
    return (y_prompt, y_sample, k_prompt, v_prompt, state4(re_p, n_p), state4(im_p, n_p),
            unpack_mem(mkt), unpack_mem(mvt), k_sample, v_sample, state4(re_s, n_s), state4(im_s, n_s))
```

```python
import functools
import math

import jax
import jax.numpy as jnp
from jax import lax
from jax.experimental import pallas as pl
from jax.experimental.pallas import tpu as pltpu

F32 = jnp.float32
BF16 = jnp.bfloat16

D_MODEL = 1024
HEAD_DIM = 64
N_HEADS = 4
ATTN_WIDTH = 512
ROT_DIM = 16
ROPE_THETA = 500000.0
SSM_WIDTH = 512
SSM_GROUP = 16
SSM_GROUPS = 32
SSM_STATE = 64
N_STATE = SSM_GROUPS * SSM_STATE
PAGE_SIZE = 128
N_MEM = 256
CROSS_WIDTH = 256
CROSS_HEAD_DIM = 64
D_FF = 4096
EPS = 1e-6
NEG_INF = -1e30
LAM_INIT = 0.8 - 0.6 * math.exp(-0.3 * 0)

LANES = 128
SUBLANES = 8
VMEM_LIMIT = 56 * 1024 * 1024

TOK_TILE = 256
PAGES_PER_CHUNK = 16


def _params(sem):
    return pltpu.CompilerParams(dimension_semantics=sem, vmem_limit_bytes=VMEM_LIMIT)


def _rms(x, g):
    ms = jnp.mean(x * x, axis=-1, keepdims=True)
    return x * lax.rsqrt(ms + EPS) * g


def _const_spec(shape):
    nd = len(shape)
    return pl.BlockSpec(shape, lambda *_: (0,) * nd)


def _rope(x, c, s1, s2):
    outs = []
    for i in range(x.shape[1] // LANES):
        xc = x[:, i * LANES:(i + 1) * LANES]
        outs.append(xc * c + pltpu.roll(xc, LANES - ROT_DIM // 2, 1) * s1 + pltpu.roll(xc, ROT_DIM // 2, 1) * s2)
    return jnp.concatenate(outs, axis=1)


def _proj_kernel(x_ref, g_ref, w_ref, c_ref, s1_ref, s2_ref, *out_refs, sample):
    h = _rms(x_ref[...], g_ref[...]).astype(BF16)
    proj = jnp.dot(h, w_ref[...], preferred_element_type=F32)
    c, s1, s2 = c_ref[...], s1_ref[...], s2_ref[...]
    q = _rope(proj[:, :ATTN_WIDTH], c, s1, s2) * (1.0 / math.sqrt(HEAD_DIM))
    k = _rope(proj[:, ATTN_WIDTH:2 * ATTN_WIDTH], c, s1, s2)
    v = proj[:, 2 * ATTN_WIDTH:3 * ATTN_WIDTH]
    if sample:
        q_ref, k_ref, v_ref, vlin_ref, u_ref = out_refs
        q_ref[...] = q
        k_ref[...] = k
        v_ref[...] = v
    else:
        qt_ref, kb_ref, kt_ref, vt_ref, vlin_ref, u_ref = out_refs
        qt_ref[0] = q.T.astype(BF16)
        kb_ref[...] = k.astype(BF16)
        kt_ref[0] = k.T
        vt_ref[0] = v.T.astype(BF16)
    for hh in range(N_HEADS):
        vlin_ref[pl.ds(hh, v.shape[0], stride=N_HEADS), :] = v[:, hh * LANES:(hh + 1) * LANES]
    u_ref[...] = proj[:, 3 * ATTN_WIDTH:]


def _project(x2d, g, w_bf, tabs, n_tab_tiles, rows_per_group, sample):
    m = x2d.shape[0]
    t = TOK_TILE
    groups = m // rows_per_group
    tiles_per_group = rows_per_group // t
    c, s1, s2 = tabs
    tab_spec = pl.BlockSpec((t, LANES), lambda i: (i % n_tab_tiles, 0))
    ft_spec = pl.BlockSpec((1, ATTN_WIDTH, t), lambda i: (i // tiles_per_group, 0, i % tiles_per_group))
    row_spec = pl.BlockSpec((t, ATTN_WIDTH), lambda i: (i, 0))
    lin_spec = pl.BlockSpec((t * N_HEADS, LANES), lambda i: (i, 0))
    row = lambda dt: jax.ShapeDtypeStruct((m, ATTN_WIDTH), dt)
    ft = lambda dt: jax.ShapeDtypeStruct((groups, ATTN_WIDTH, rows_per_group), dt)
    lin = jax.ShapeDtypeStruct((m * N_HEADS, LANES), F32)
    if sample:
        out_specs = [row_spec, row_spec, row_spec, lin_spec, row_spec]
        out_shape = [row(F32), row(F32), row(F32), lin, row(F32)]
    else:
        out_specs = [ft_spec, row_spec, ft_spec, ft_spec, lin_spec, row_spec]
        out_shape = [ft(BF16), row(BF16), ft(F32), ft(BF16), lin, row(F32)]
    return pl.pallas_call(
        functools.partial(_proj_kernel, sample=sample),
        grid=(m // t,),
        in_specs=[pl.BlockSpec((t, D_MODEL), lambda i: (i, 0)),
                  _const_spec((1, D_MODEL)),
                  _const_spec((D_MODEL, 4 * ATTN_WIDTH)),
                  tab_spec, tab_spec, tab_spec],
        out_specs=out_specs,
        out_shape=out_shape,
        compiler_params=_params(("arbitrary",)),
        name="proj",
    )(x2d, g, w_bf, c, s1, s2)


def _rope_tables(pos):
    half = ROT_DIM // 2
    inv_freq = jnp.float32(ROPE_THETA) ** (-jnp.arange(half, dtype=F32) * 2.0 / ROT_DIM)
    ang = pos.astype(F32)[:, None] * inv_freq[None, :]
    cos, sin = jnp.cos(ang), jnp.sin(ang)
    n = pos.shape[0]
    pad = jnp.zeros((n, HEAD_DIM - ROT_DIM), F32)
    c = jnp.concatenate([cos, cos, pad + 1.0], axis=1)
    s1 = jnp.concatenate([-sin, jnp.zeros_like(sin), pad], axis=1)
    s2 = jnp.concatenate([jnp.zeros_like(sin), sin, pad], axis=1)
    tile2 = lambda a: jnp.concatenate([a, a], axis=1)
    return tile2(c), tile2(s1), tile2(s2)


def _lam(lq1, lk1, lq2, lk2):
    return (jnp.exp(jnp.sum(lq1 * lk1, keepdims=True)) - jnp.exp(jnp.sum(lq2 * lk2, keepdims=True))
            + LAM_INIT)


def _subln(o, gain):
    ms = jnp.mean(o * o, axis=-1, keepdims=True)
    return o * lax.rsqrt(ms + EPS) * gain * (1.0 - LAM_INIT)


ATTN_COLS = 256


def _prompt_attn_kernel(qt_ref, k_ref, vt_ref, lq1, lk1, lq2, lk2, gain_ref, o_ref,
                        q2_s, m_s, l_s, acc_s, *, tq):
    qi = pl.program_id(2)
    qt = qt_ref[0]
    feat = lax.broadcasted_iota(jnp.int32, qt.shape, 0)
    zero = jnp.zeros_like(qt)
    q2_s[:, 0:tq] = jnp.where(feat < HEAD_DIM, qt, zero)
    q2_s[:, tq:2 * tq] = jnp.where(feat >= HEAD_DIM, qt, zero)
    m_s[...] = jnp.full(m_s.shape, NEG_INF, F32)
    l_s[...] = jnp.zeros(l_s.shape, F32)
    acc_s[...] = jnp.zeros(acc_s.shape, F32)

    def step(ki, masked):
        start = pl.multiple_of(ki * tq, tq)
        k = k_ref[pl.ds(start, tq), :]
        vt = vt_ref[0, :, pl.ds(start, tq)]
        groups = [slice(g * ATTN_COLS, (g + 1) * ATTN_COLS) for g in range(2 * tq // ATTN_COLS)]
        n_keys = [(cols.stop - 1) % tq + 1 if masked else tq for cols in groups]
        scores = [jnp.dot(k[:nk], q2_s[:, cols], preferred_element_type=F32) for cols, nk in zip(groups, n_keys)]
        probs, stats = [], []
        for cols, s in zip(groups, scores):
            if masked:
                key = lax.broadcasted_iota(jnp.int32, s.shape, 0)
                qry = (lax.broadcasted_iota(jnp.int32, s.shape, 1) + cols.start) % tq
                s = jnp.where(key <= qry, s, NEG_INF)
            m_old = m_s[:, cols]
            m_new = jnp.maximum(m_old, jnp.max(s, axis=0, keepdims=True))
            alpha = jnp.exp(m_old - m_new)
            p = jnp.exp(s - m_new)
            probs.append(p.astype(BF16))
            stats.append((m_new, alpha, alpha * l_s[:, cols] + jnp.sum(p, axis=0, keepdims=True)))
        for cols, nk, p, (m_new, alpha, l_new) in zip(groups, n_keys, probs, stats):
            acc_s[:, cols] = alpha * acc_s[:, cols] + jnp.dot(vt[:, :nk], p, preferred_element_type=F32)
            m_s[:, cols] = m_new
            l_s[:, cols] = l_new

    def body(ki, carry):
        step(ki, False)
        return carry

    lax.fori_loop(0, qi, body, 0)
    step(qi, True)

    lam = _lam(lq1[...], lk1[...], lq2[...], lk2[...])
    o = acc_s[...] / l_s[...]
    o = o[:, 0:tq] - lam * o[:, tq:2 * tq]
    ms = jnp.mean(o * o, axis=0, keepdims=True)
    o = o * lax.rsqrt(ms + EPS) * gain_ref[...] * (1.0 - LAM_INIT)
    o_ref[...] = o.T.astype(o_ref.dtype)


def _prompt_attention(qt_bf, k_bf, vt_bf, lam_params, gain_col, n_batch, seq, tq=512):
    nq = seq // tq
    lam_spec = _const_spec((1, HEAD_DIM))
    return pl.pallas_call(
        functools.partial(_prompt_attn_kernel, tq=tq),
        grid=(n_batch, N_HEADS, nq),
        in_specs=[pl.BlockSpec((1, LANES, tq), lambda n, h, i: (n, h, i)),
                  pl.BlockSpec((seq, LANES), lambda n, h, i: (n, h)),
                  pl.BlockSpec((1, LANES, seq), lambda n, h, i: (n, h, 0)),
                  lam_spec, lam_spec, lam_spec, lam_spec,
                  _const_spec((LANES, 1))],
        out_specs=pl.BlockSpec((tq, LANES), lambda n, h, i: (n * nq + i, h)),
        out_shape=jax.ShapeDtypeStruct((n_batch * seq, ATTN_WIDTH), BF16),
        scratch_shapes=[pltpu.VMEM((LANES, 2 * tq), BF16),
                        pltpu.VMEM((1, 2 * tq), F32),
                        pltpu.VMEM((1, 2 * tq), F32),
                        pltpu.VMEM((LANES, 2 * tq), F32)],
        compiler_params=_params(("arbitrary", "arbitrary", "arbitrary")),
        name="prompt_attn",
    )(qt_bf, k_bf, vt_bf, *lam_params, gain_col)


SEQS_PER_STEP = 2


def _paged_mlp_kernel(pt_ref, q_ref, knew_ref, vnew_ref, lq1, lk1, lq2, lk2, gain_ref,
                      x_ref, g_ref, gf_ref, ck_hbm, cv_hbm, wu_hbm, wd_hbm,
                      o_ref, y_ref, kbuf, vbuf, sem, wu_s, wd_s, wsem, h_s, acc_s,
                      *, n_steps, n_chunks, t_new):
    g_idx = pl.program_id(0)
    ppc = PAGES_PER_CHUNK
    spg = SEQS_PER_STEP
    rows = 2 * N_HEADS * t_new
    n_iter = spg * n_chunks
    ff_chunk = D_FF // n_iter

    def page_copies(seq, chunk, slot):
        cps = []
        for pg in range(ppc):
            page = pt_ref[chunk * ppc + pg, seq]
            cps.append(pltpu.make_async_copy(ck_hbm.at[page], kbuf.at[slot, pg], sem.at[0, slot]))
            cps.append(pltpu.make_async_copy(cv_hbm.at[page], vbuf.at[slot, pg], sem.at[1, slot]))
        return cps

    def weight_copies():
        return [pltpu.make_async_copy(wu_hbm, wu_s, wsem.at[0]), pltpu.make_async_copy(wd_hbm, wd_s, wsem.at[1])]

    @pl.when(g_idx == 0)
    def _():
        for cp in page_copies(0, 0, 0):
            cp.start()
        for cp in weight_copies():
            cp.start()
        for cp in weight_copies():
            cp.wait()

    def mlp_part(j):
        if j == 0:
            x = x_ref[...]
            h_s[...] = _rms(x, g_ref[...]).astype(BF16)
            acc_s[...] = x
        z = jnp.dot(h_s[...], wu_s[:, j * ff_chunk:(j + 1) * ff_chunk], preferred_element_type=F32)
        a = jnp.square(jnp.maximum(z, 0.0)).astype(BF16)
        acc_s[...] += jnp.dot(a, wd_s[j * ff_chunk:(j + 1) * ff_chunk, :], preferred_element_type=F32)
        if j == n_iter - 1:
            y_ref[...] = _rms(acc_s[...], gf_ref[...])

    lam = _lam(lq1[...], lk1[...], lq2[...], lk2[...])
    gain = gain_ref[...]

    for j in range(n_iter):
        i, c = divmod(j, n_chunks)
        seq = g_idx * spg + i
        slot = j % 2
        nslot = (j + 1) % 2
        if j + 1 < n_iter:
            i2, c2 = divmod(j + 1, n_chunks)
            for cp in page_copies(g_idx * spg + i2, c2, nslot):
                cp.start()
        else:
            @pl.when(g_idx + 1 < n_steps)
            def _():
                for cp in page_copies((g_idx + 1) * spg, 0, nslot):
                    cp.start()
        mlp_part(j)
        for cp in page_copies(seq, c, slot):
            cp.wait()
        if c == 0:
            q = q_ref[i * t_new:(i + 1) * t_new, :]
            qt = jnp.concatenate([q] * (2 * N_HEADS), axis=0)
            r_hj = lax.broadcasted_iota(jnp.int32, qt.shape, 0) // t_new
            c_hj = lax.broadcasted_iota(jnp.int32, qt.shape, 1) // HEAD_DIM
            qbd = jnp.where(r_hj == c_hj, qt, 0.0).astype(BF16)
            m = jnp.full((rows, 1), NEG_INF, F32)
            l = jnp.zeros((rows, 1), F32)
            acc = jnp.zeros((rows, ATTN_WIDTH), F32)
        kt = jnp.concatenate([kbuf[slot, pg].astype(BF16) for pg in range(ppc)], axis=1)
        s = jnp.dot(qbd, kt, preferred_element_type=F32)
        m_new = jnp.maximum(m, jnp.max(s, axis=-1, keepdims=True))
        alpha = jnp.exp(m - m_new)
        p = jnp.exp(s - m_new)
        l = alpha * l + jnp.sum(p, axis=-1, keepdims=True)
        v = jnp.concatenate(
            [jnp.concatenate([vbuf[slot, pg, pl.ds(hh, PAGE_SIZE, stride=N_HEADS), :] for hh in range(N_HEADS)],
                             axis=1) for pg in range(ppc)], axis=0).astype(BF16)
        acc = alpha * acc + jnp.dot(p.astype(BF16), v, preferred_element_type=F32)
        m = m_new
        if c < n_chunks - 1:
            continue

        pad = jnp.zeros((t_new, ATTN_WIDTH), F32)
        knew = jnp.concatenate([knew_ref[i * t_new:(i + 1) * t_new, :], pad], axis=0).astype(BF16)
        vnew = jnp.concatenate([vnew_ref[i * t_new:(i + 1) * t_new, :], pad], axis=0).astype(BF16)
        s = lax.dot_general(qbd, knew, (((1,), (1,)), ((), ())), preferred_element_type=F32)
        row_t = lax.broadcasted_iota(jnp.int32, s.shape, 0) % t_new
        col_t = lax.broadcasted_iota(jnp.int32, s.shape, 1)
        s = jnp.where(col_t <= row_t, s, NEG_INF)
        m_new = jnp.maximum(m, jnp.max(s, axis=-1, keepdims=True))
        alpha = jnp.exp(m - m_new)
        p = jnp.exp(s - m_new)
        l = alpha * l + jnp.sum(p, axis=-1, keepdims=True)
        acc = alpha * acc + jnp.dot(p.astype(BF16), vnew, preferred_element_type=F32)
        o_all = acc / l
        outs = []
        for hh in range(N_HEADS):
            r0 = hh * 2 * t_new
            blk = o_all[r0:r0 + 2 * t_new, hh * LANES:(hh + 1) * LANES]
            o = blk[0:t_new, :] - lam * blk[t_new:2 * t_new, :]
            outs.append(_subln(o, gain))
        o_ref[i * t_new:(i + 1) * t_new, :] = jnp.concatenate(outs, axis=1)


def _paged_mlp(pt_t, q, knew, vnew, lam_params, gain, ck, cv, x2d, g, gf, wu, wd, n_seq, t_new):
    n_pages = pt_t.shape[0]
    n_chunks = n_pages // PAGES_PER_CHUNK
    n_steps = n_seq // SEQS_PER_STEP
    m = x2d.shape[0]
    t = m // n_steps
    assert n_seq % SEQS_PER_STEP == 0 and m % n_steps == 0 and t % SUBLANES == 0
    assert D_FF % (SEQS_PER_STEP * n_chunks) == 0 and (SEQS_PER_STEP * n_chunks) % 2 == 0
    cmap = lambda s, pt: (0, 0)
    lam_spec = pl.BlockSpec((1, HEAD_DIM), cmap)
    row_spec = pl.BlockSpec((SEQS_PER_STEP * t_new, ATTN_WIDTH), lambda s, pt: (s, 0))
    x_spec = pl.BlockSpec((t, D_MODEL), lambda s, pt: (s, 0))
    vec_spec = pl.BlockSpec((1, D_MODEL), cmap)
    any_spec = pl.BlockSpec(memory_space=pl.ANY)
    feat = ck.shape[1]
    grid_spec = pltpu.PrefetchScalarGridSpec(
        num_scalar_prefetch=1,
        grid=(n_steps,),
        in_specs=[row_spec, row_spec, row_spec,
                  lam_spec, lam_spec, lam_spec, lam_spec,
                  pl.BlockSpec((1, LANES), cmap),
                  x_spec, vec_spec, vec_spec,
                  any_spec, any_spec, any_spec, any_spec],
        out_specs=[row_spec, x_spec],
        scratch_shapes=[pltpu.VMEM((2, PAGES_PER_CHUNK, feat, PAGE_SIZE), F32),
                        pltpu.VMEM((2, PAGES_PER_CHUNK, PAGE_SIZE * N_HEADS, LANES), F32),
                        pltpu.SemaphoreType.DMA((2, 2)),
                        pltpu.VMEM((D_MODEL, D_FF), BF16),
                        pltpu.VMEM((D_FF, D_MODEL), BF16),
                        pltpu.SemaphoreType.DMA((2,)),
                        pltpu.VMEM((t, D_MODEL), BF16),
                        pltpu.VMEM((t, D_MODEL), F32)],
    )
    return pl.pallas_call(
        functools.partial(_paged_mlp_kernel, n_steps=n_steps, n_chunks=n_chunks, t_new=t_new),
        grid_spec=grid_spec,
        out_shape=[jax.ShapeDtypeStruct((n_seq * t_new, ATTN_WIDTH), F32),
                   jax.ShapeDtypeStruct((m, D_MODEL), F32)],
        compiler_params=_params(("arbitrary",)),
        name="paged_mlp",
    )(pt_t, q, knew, vnew, *lam_params, gain, x2d, g, gf, ck, cv, wu, wd)


HALF_STATE = N_STATE // 2
S5_TIME_TILE = 32


def _s5_tail(x_bf, u, cc_ref, d_ref, wg_ref, bg_ref):
    y = jnp.concatenate(
        [jnp.dot(x_bf[:, hf * N_STATE:(hf + 1) * N_STATE], cc_ref[hf], preferred_element_type=F32) for hf in range(2)],
        axis=1) + d_ref[...] * u
    g = 0.5 * y * (1.0 + lax.erf(y * (1.0 / math.sqrt(2.0))))
    z = jnp.dot(g.astype(BF16), wg_ref[...], preferred_element_type=F32) + bg_ref[...]
    return g * (1.0 / (1.0 + jnp.exp(-z)))


def _s5_bu(u_bf, bb_ref, bu_s):
    half_u = SSM_WIDTH // 2
    for hf in range(2):
        bu_s[:, hf * N_STATE:(hf + 1) * N_STATE] = jnp.dot(u_bf[:, hf * half_u:(hf + 1) * half_u], bb_ref[hf],
                                                           preferred_element_type=F32)


def _s5_prompt_kernel(u_ref, perm_ref, permt_ref, bb_ref, cc_ref, lb_ref, d_ref, wg_ref, bg_ref,
                      ys_ref, ht_ref, bu_s, carry_s):
    nb, tt = u_ref.shape[0], u_ref.shape[1]
    rows = nb * tt

    @pl.when(pl.program_id(0) == 0)
    def _():
        carry_s[...] = jnp.zeros(carry_s.shape, F32)

    u = u_ref[...].reshape(rows, SSM_WIDTH)
    hi = u.astype(BF16)
    r1 = u - hi.astype(F32)
    mid = r1.astype(BF16)
    lo = (r1 - mid.astype(F32)).astype(BF16)
    perm = perm_ref[...]
    u_hi = jnp.dot(perm, hi, preferred_element_type=F32)
    u_tb = u_hi + jnp.dot(perm, mid, preferred_element_type=F32) + jnp.dot(perm, lo, preferred_element_type=F32)
    _s5_bu(u_hi.astype(BF16), bb_ref, bu_s)

    for hf in range(2):
        c0 = hf * N_STATE
        lbr = lb_ref[0, :, hf * HALF_STATE:(hf + 1) * HALF_STATE]
        lbi = lb_ref[1, :, hf * HALF_STATE:(hf + 1) * HALF_STATE]

        def body(t, carry, c0=c0, lbr=lbr, lbi=lbi):
            xr, xi = carry
            r0 = pl.multiple_of(t * nb, nb)
            br = bu_s[pl.ds(r0, nb), c0:c0 + HALF_STATE]
            bi = bu_s[pl.ds(r0, nb), c0 + HALF_STATE:c0 + N_STATE]
            xr, xi = lbr * xr - lbi * xi + br, lbr * xi + lbi * xr + bi
            bu_s[pl.ds(r0, nb), c0:c0 + HALF_STATE] = xr
            bu_s[pl.ds(r0, nb), c0 + HALF_STATE:c0 + N_STATE] = xi
            return xr, xi

        xr, xi = lax.fori_loop(0, tt, body, (carry_s[:, c0:c0 + HALF_STATE], carry_s[:, c0 + HALF_STATE:c0 + N_STATE]),
                               unroll=4)
        carry_s[:, c0:c0 + HALF_STATE] = xr
        carry_s[:, c0 + HALF_STATE:c0 + N_STATE] = xi
    ht_ref[...] = carry_s[...]

    ys_tb = _s5_tail(bu_s[...].astype(BF16), u_tb, cc_ref, d_ref, wg_ref, bg_ref).astype(BF16)
    ys = jnp.dot(permt_ref[...], ys_tb, preferred_element_type=F32)
    ys_ref[...] = ys.astype(ys_ref.dtype).reshape(nb, tt, SSM_WIDTH)


def _s5_prompt(u3, ssm):
    bb, cc, lb_tab, a_tab, pw_tab, d_row, wg, bg = ssm
    nb, seq = u3.shape[0], u3.shape[1]
    tt = S5_TIME_TILE
    rows = nb * tt
    r = jnp.arange(rows)
    perm = jax.nn.one_hot((r % nb) * tt + r // nb, rows, dtype=BF16)
    blk = pl.BlockSpec((nb, tt, SSM_WIDTH), lambda i: (0, i, 0))
    return pl.pallas_call(
        _s5_prompt_kernel,
        grid=(seq // tt,),
        in_specs=[blk, _const_spec((rows, rows)), _const_spec((rows, rows)),
                  _const_spec(bb.shape), _const_spec(cc.shape), _const_spec(lb_tab.shape),
                  _const_spec(d_row.shape), _const_spec(wg.shape), _const_spec(bg.shape)],
        out_specs=[blk, _const_spec((nb, 2 * N_STATE))],
        out_shape=[jax.ShapeDtypeStruct((nb, seq, SSM_WIDTH), BF16),
                   jax.ShapeDtypeStruct((nb, 2 * N_STATE), F32)],
        scratch_shapes=[pltpu.VMEM((rows, 2 * N_STATE), F32), pltpu.VMEM((nb, 2 * N_STATE), F32)],
        compiler_params=_params(("arbitrary",)),
        name="s5_prompt",
    )(u3, perm, perm.T, bb, cc, lb_tab, d_row, wg, bg)


def _s5_sample_kernel(u_ref, h0_ref, bb_ref, cc_ref, a_ref, pw_ref, d_ref, wg_ref, bg_ref,
                      ys_ref, ht_ref, bu_s):
    t = u_ref.shape[0]
    u = u_ref[...]
    _s5_bu(u.astype(BF16), bb_ref, bu_s)

    def cmul_add(xr, xi, ar, ai, sr, si):
        return xr + ar * sr - ai * si, xi + ar * si + ai * sr

    def body(b, carry):
        r0 = pl.multiple_of(b * SUBLANES, SUBLANES)
        init = h0_ref[pl.ds(b, 1), :]
        for hf in range(2):
            c0 = hf * N_STATE
            st = slice(hf * HALF_STATE, (hf + 1) * HALF_STATE)
            xr = bu_s[pl.ds(r0, SUBLANES), c0:c0 + HALF_STATE]
            xi = bu_s[pl.ds(r0, SUBLANES), c0 + HALF_STATE:c0 + N_STATE]
            for k, shift in enumerate((1, 2, 4)):
                sr = pltpu.roll(xr, shift, 0)
                si = pltpu.roll(xi, shift, 0)
                xr, xi = cmul_add(xr, xi, a_ref[k, 0, :, st], a_ref[k, 1, :, st], sr, si)
            cr = jnp.broadcast_to(init[:, c0:c0 + HALF_STATE], xr.shape)
            ci = jnp.broadcast_to(init[:, c0 + HALF_STATE:c0 + N_STATE], xi.shape)
            xr, xi = cmul_add(xr, xi, pw_ref[0, :, st], pw_ref[1, :, st], cr, ci)
            bu_s[pl.ds(r0, SUBLANES), c0:c0 + HALF_STATE] = xr
            bu_s[pl.ds(r0, SUBLANES), c0 + HALF_STATE:c0 + N_STATE] = xi
            ht_ref[pl.ds(b, 1), c0:c0 + HALF_STATE] = xr[SUBLANES - 1:SUBLANES, :]
            ht_ref[pl.ds(b, 1), c0 + HALF_STATE:c0 + N_STATE] = xi[SUBLANES - 1:SUBLANES, :]
        return carry

    lax.fori_loop(0, t // SUBLANES, body, 0)
    ys_ref[...] = _s5_tail(bu_s[...].astype(BF16), u, cc_ref, d_ref, wg_ref, bg_ref).astype(ys_ref.dtype)


def _s5_sample(u, h0, ssm):
    bb, cc, lb_tab, a_tab, pw_tab, d_row, wg, bg = ssm
    t = TOK_TILE
    m = u.shape[0]
    h_spec = pl.BlockSpec((t // SUBLANES, 2 * N_STATE), lambda i: (i, 0))
    return pl.pallas_call(
        _s5_sample_kernel,
        grid=(m // t,),
        in_specs=[pl.BlockSpec((t, SSM_WIDTH), lambda i: (i, 0)), h_spec,
                  _const_spec(bb.shape), _const_spec(cc.shape), _const_spec(a_tab.shape), _const_spec(pw_tab.shape),
                  _const_spec(d_row.shape), _const_spec(wg.shape), _const_spec(bg.shape)],
        out_specs=[pl.BlockSpec((t, SSM_WIDTH), lambda i: (i, 0)), h_spec],
        out_shape=[jax.ShapeDtypeStruct((m, SSM_WIDTH), BF16),
                   jax.ShapeDtypeStruct((m // SUBLANES, 2 * N_STATE), F32)],
        scratch_shapes=[pltpu.VMEM((t, 2 * N_STATE), F32)],
        compiler_params=_params(("arbitrary",)),
        name="s5_sample",
    )(u, h0, bb, cc, a_tab, pw_tab, d_row, wg, bg)


def _state_pack(re, im):
    return jnp.concatenate([re[:, :HALF_STATE], im[:, :HALF_STATE], re[:, HALF_STATE:], im[:, HALF_STATE:]], axis=1)


def _state_unpack(x):
    re = jnp.concatenate([x[:, 0:HALF_STATE], x[:, N_STATE:N_STATE + HALF_STATE]], axis=1)
    im = jnp.concatenate([x[:, HALF_STATE:N_STATE], x[:, N_STATE + HALF_STATE:]], axis=1)
    return re, im


def _ssm_tables(a_re, a_im, log_dt, b_re, b_im, c_re, c_im, d_skip, w_glu, b_glu):
    delta = jnp.exp(log_dt)[:, None]
    mag = jnp.exp(a_re * delta)
    ang = a_im * delta
    lb_re = mag * jnp.cos(ang)
    lb_im = mag * jnp.sin(ang)
    den = a_re * a_re + a_im * a_im
    num_re = lb_re - 1.0
    cz_re = (num_re * a_re + lb_im * a_im) / den
    cz_im = (lb_im * a_re - num_re * a_im) / den
    bb_re = cz_re[..., None] * b_re - cz_im[..., None] * b_im
    bb_im = cz_re[..., None] * b_im + cz_im[..., None] * b_re
    gh = SSM_GROUPS // 2
    eye = jnp.eye(gh, dtype=F32)
    bd_in = lambda w: jnp.einsum('gsp,gh->gphs', w, eye).reshape(gh * SSM_GROUP, HALF_STATE)
    bd_out = lambda w: jnp.einsum('gps,gh->gshp', w, eye).reshape(HALF_STATE, gh * SSM_GROUP)
    halves = lambda w: (w[:gh], w[gh:])
    bb = jnp.stack([jnp.concatenate([bd_in(r), bd_in(i)], axis=1)
                    for r, i in zip(halves(bb_re), halves(bb_im))]).astype(BF16)
    cc = jnp.stack([jnp.concatenate([bd_out(r), bd_out(-i)], axis=0)
                    for r, i in zip(halves(c_re), halves(c_im))]).astype(BF16)

    def cmul(ar, ai, br, bi):
        return ar * br - ai * bi, ar * bi + ai * br

    l1 = (lb_re.reshape(N_STATE), lb_im.reshape(N_STATE))
    pows = [l1]
    for _ in range(SUBLANES - 1):
        pows.append(cmul(*pows[-1], *l1))
    rows = jnp.arange(SUBLANES)[:, None]
    a_tab = jnp.stack([jnp.stack([jnp.where(rows >= sh, pows[sh - 1][0][None, :], 0.0),
                                  jnp.where(rows >= sh, pows[sh - 1][1][None, :], 0.0)]) for sh in (1, 2, 4)])
    pw_tab = jnp.stack([jnp.stack([p[0] for p in pows]), jnp.stack([p[1] for p in pows])])
    lb_tab = jnp.stack([jnp.broadcast_to(l1[0][None, :], (SUBLANES, N_STATE)),
                        jnp.broadcast_to(l1[1][None, :], (SUBLANES, N_STATE))])
    return (bb, cc, lb_tab, a_tab.astype(F32), pw_tab, d_skip.reshape(1, SSM_WIDTH),
            w_glu.astype(BF16), b_glu.reshape(1, SSM_WIDTH))


def _memkv_kernel(mem_ref, g_ref, wk_ref, wv_ref, kt_ref, vt_ref, ktb_ref, vtb_ref):
    mn = _rms(mem_ref[...], g_ref[...]).astype(BF16)
    kt = jnp.dot(mn, wk_ref[...], preferred_element_type=F32).T
    vt = jnp.dot(mn, wv_ref[...], preferred_element_type=F32).T
    kt_ref[0] = kt
    vt_ref[0] = vt
    ktb_ref[0] = kt.astype(BF16)
    vtb_ref[0] = vt.astype(BF16)


def _memory_kv(mem2d, g, wk, wv, n_batch):
    spec = pl.BlockSpec((1, CROSS_WIDTH, N_MEM), lambda n: (n, 0, 0))
    f = jax.ShapeDtypeStruct((n_batch, CROSS_WIDTH, N_MEM), F32)
    b = jax.ShapeDtypeStruct((n_batch, CROSS_WIDTH, N_MEM), BF16)
    return pl.pallas_call(
        _memkv_kernel,
        grid=(n_batch,),
        in_specs=[pl.BlockSpec((N_MEM, D_MODEL), lambda n: (n, 0)), _const_spec((1, D_MODEL)),
                  _const_spec((D_MODEL, CROSS_WIDTH)), _const_spec((D_MODEL, CROSS_WIDTH))],
        out_specs=[spec, spec, spec, spec],
        out_shape=[f, f, b, b],
        compiler_params=_params(("arbitrary",)),
        name="memory_kv",
    )(mem2d, g, wk, wv)


def _cross_attend(hq, kt, vt, n_q):
    heads = CROSS_WIDTH // CROSS_HEAD_DIM
    stacked = jnp.concatenate([hq] * heads, axis=0)
    r_h = lax.broadcasted_iota(jnp.int32, stacked.shape, 0) // n_q
    c_h = lax.broadcasted_iota(jnp.int32, stacked.shape, 1) // CROSS_HEAD_DIM
    own = r_h == c_h
    s = jnp.dot(jnp.where(own, stacked, 0.0).astype(BF16), kt, preferred_element_type=F32)
    p = jnp.exp(s - jnp.max(s, axis=-1, keepdims=True))
    p = p / jnp.sum(p, axis=-1, keepdims=True)
    full = lax.dot_general(p.astype(BF16), vt, (((1,), (1,)), ((), ())), preferred_element_type=F32)
    full = jnp.where(own, full, 0.0)
    out = full[0:n_q, :]
    for hh in range(1, heads):
        out = out + full[hh * n_q:(hh + 1) * n_q, :]
    return out


def _mix_cross_kernel(x_ref, o_ref, ys_ref, wo_ref, g_ref, wq_ref, kt_ref, vt_ref, wco_ref, out_ref, *, seq_rows):
    o = o_ref[...].astype(BF16)
    mix = (jnp.dot(o, wo_ref[0:ATTN_WIDTH, :], preferred_element_type=F32)
           + jnp.dot(ys_ref[...], wo_ref[ATTN_WIDTH:, :], preferred_element_type=F32))
    x1 = x_ref[...] + mix
    hq = jnp.dot(_rms(x1, g_ref[...]).astype(BF16), wq_ref[...], preferred_element_type=F32)
    hq = hq * (1.0 / math.sqrt(CROSS_HEAD_DIM))
    pieces = []
    for j in range(x1.shape[0] // seq_rows):
        pieces.append(_cross_attend(hq[j * seq_rows:(j + 1) * seq_rows, :], kt_ref[j].astype(BF16),
                                    vt_ref[j].astype(BF16), seq_rows))
    oc = jnp.concatenate(pieces, axis=0)
    out_ref[...] = x1 + jnp.dot(oc.astype(BF16), wco_ref[...], preferred_element_type=F32)


def _mix_cross(x2d, o, ys, wo, g, wq, kt, vt, wco, seq_rows, t, tiles_per_mem):
    m = x2d.shape[0]
    mem_spec = pl.BlockSpec((t // seq_rows, CROSS_WIDTH, N_MEM), lambda i: (i // tiles_per_mem, 0, 0))
    return pl.pallas_call(
        functools.partial(_mix_cross_kernel, seq_rows=seq_rows),
        grid=(m // t,),
        in_specs=[pl.BlockSpec((t, D_MODEL), lambda i: (i, 0)),
                  pl.BlockSpec((t, ATTN_WIDTH), lambda i: (i, 0)),
                  pl.BlockSpec((t, SSM_WIDTH), lambda i: (i, 0)),
                  _const_spec((D_MODEL, D_MODEL)), _const_spec((1, D_MODEL)),
                  _const_spec((D_MODEL, CROSS_WIDTH)), mem_spec, mem_spec,
                  _const_spec((CROSS_WIDTH, D_MODEL))],
        out_specs=pl.BlockSpec((t, D_MODEL), lambda i: (i, 0)),
        out_shape=jax.ShapeDtypeStruct((m, D_MODEL), F32),
        compiler_params=_params(("arbitrary",)),
        name="mix_cross",
    )(x2d, o, ys, wo, g, wq, kt, vt, wco)


def _mlp_kernel(x_ref, g_ref, wu_ref, wd_ref, gf_ref, y_ref, *, ff_chunk):
    x = x_ref[...]
    h = _rms(x, g_ref[...]).astype(BF16)
    acc = x
    for c in range(D_FF // ff_chunk):
        z = jnp.dot(h, wu_ref[:, c * ff_chunk:(c + 1) * ff_chunk], preferred_element_type=F32)
        a = jnp.square(jnp.maximum(z, 0.0)).astype(BF16)
        acc = acc + jnp.dot(a, wd_ref[c * ff_chunk:(c + 1) * ff_chunk, :], preferred_element_type=F32)
    y_ref[...] = _rms(acc, gf_ref[...])


def _mlp(x2d, g, wu, wd, gf, ff_chunk=1024):
    m = x2d.shape[0]
    t = TOK_TILE
    return pl.pallas_call(
        functools.partial(_mlp_kernel, ff_chunk=ff_chunk),
        grid=(m // t,),
        in_specs=[pl.BlockSpec((t, D_MODEL), lambda i: (i, 0)), _const_spec((1, D_MODEL)),
                  _const_spec((D_MODEL, D_FF)), _const_spec((D_FF, D_MODEL)), _const_spec((1, D_MODEL))],
        out_specs=pl.BlockSpec((t, D_MODEL), lambda i: (i, 0)),
        out_shape=jax.ShapeDtypeStruct((m, D_MODEL), F32),
        compiler_params=_params(("arbitrary",)),
        name="mlp",
    )(x2d, g, wu, wd, gf)


def kernel(x_prompt, x_sample, mem_prompt, cache_k, cache_v, page_table, state_ssm_re, state_ssm_im, cache_mem_k, cache_mem_v, norm_mix, w_in, lambda_q1, lambda_k1, lambda_q2, lambda_k2, subln_gain, ssm_a_re, ssm_a_im, ssm_log_dt, ssm_b_re, ssm_b_im, ssm_c_re, ssm_c_im, ssm_d, w_glu, b_glu, w_out, norm_cross, norm_mem, w_cq, w_ck, w_cv, w_co, norm_mlp, w_up, w_down, final_norm):
    n_p, t_p = x_prompt.shape[0], x_prompt.shape[1]
    n_s, t_s = x_sample.shape[0], x_sample.shape[1]
    n_pool = cache_k.shape[1]
    past = page_table.shape[1] * PAGE_SIZE
    assert cache_k.shape[0] == 1 and t_s == SUBLANES and n_p == SUBLANES
    assert t_p % TOK_TILE == 0 and (n_s * t_s) % TOK_TILE == 0

    l = 0
    w_in_b = w_in[l].astype(BF16)
    w_out_b = w_out[l].astype(BF16)
    w_cq_b, w_ck_b, w_cv_b, w_co_b = (w[l].astype(BF16) for w in (w_cq, w_ck, w_cv, w_co))
    w_up_b, w_down_b = w_up[l].astype(BF16), w_down[l].astype(BF16)
    lam_params = (lambda_q1, lambda_k1, lambda_q2, lambda_k2)
    ssm = _ssm_tables(ssm_a_re[l], ssm_a_im[l], ssm_log_dt[l], ssm_b_re[l], ssm_b_im[l],
                      ssm_c_re[l], ssm_c_im[l], ssm_d[l], w_glu[l], b_glu[l])
    final_g = final_norm.reshape(1, D_MODEL)

    xp = x_prompt.reshape(n_p * t_p, D_MODEL)
    tabs_p = _rope_tables(jnp.arange(t_p, dtype=jnp.int32))
    qt_p, kb_p, kt_p, vt_p, vlin_p, u_p = _project(xp, norm_mix, w_in_b, tabs_p, t_p // TOK_TILE, t_p, False)
    o_p = _prompt_attention(qt_p, kb_p, vt_p, lam_params, subln_gain.reshape(LANES, 1), n_p, t_p)
    ys_p, ht_p = _s5_prompt(u_p.reshape(n_p, t_p, SSM_WIDTH), ssm)
    mkt, mvt, mktb, mvtb = _memory_kv(mem_prompt.reshape(n_p * N_MEM, D_MODEL), norm_mem, w_ck_b, w_cv_b, n_p)
    x2_p = _mix_cross(xp, o_p, ys_p.reshape(n_p * t_p, SSM_WIDTH), w_out_b, norm_cross, w_cq_b, mktb, mvtb, w_co_b,
                      TOK_TILE, TOK_TILE, t_p // TOK_TILE)

    xs = x_sample.reshape(n_s * t_s, D_MODEL)
    pos_s = past + (jnp.arange(TOK_TILE, dtype=jnp.int32) % t_s)
    tabs_s = _rope_tables(pos_s)
    q_s, k_s, v_s, vlin_s, u_s = _project(xs, norm_mix, w_in_b, tabs_s, 1, n_s * t_s, True)
    ck = jnp.transpose(cache_k, (0, 1, 3, 4, 5, 2)).reshape(n_pool, ATTN_WIDTH, PAGE_SIZE)
    cv = cache_v.reshape(n_pool, PAGE_SIZE * N_HEADS, LANES)
    o_s, y_p = _paged_mlp(page_table.T, q_s, k_s, v_s, lam_params, subln_gain, ck, cv,
                          x2_p, norm_mlp, final_g, w_up_b, w_down_b, n_s, t_s)
    h0_s = _state_pack(state_ssm_re[l].reshape(n_s, N_STATE), state_ssm_im[l].reshape(n_s, N_STATE))
    ys_s, ht_s = _s5_sample(u_s, h0_s, ssm)
    cmk = jnp.transpose(cache_mem_k[l], (0, 2, 3, 1)).reshape(n_s, CROSS_WIDTH, N_MEM)
    cmv = jnp.transpose(cache_mem_v[l], (0, 2, 3, 1)).reshape(n_s, CROSS_WIDTH, N_MEM)
    x2_s = _mix_cross(xs, o_s, ys_s, w_out_b, norm_cross, w_cq_b, cmk, cmv, w_co_b, t_s, 128, 1)
    y_s = _mlp(x2_s, norm_mlp, w_up_b, w_down_b, final_g)

    y_prompt = y_p.reshape(n_p, t_p, D_MODEL)
    y_sample = y_s.reshape(n_s, t_s, D_MODEL)
    k_prompt = jnp.transpose(kt_p.reshape(1, n_p, N_HEADS, 2, HEAD_DIM, t_p), (0, 1, 5, 2, 3, 4))
    v_prompt = vlin_p.reshape(1, n_p, t_p, N_HEADS, 2 * HEAD_DIM)
    state4 = lambda a, n: a.reshape(1, n, SSM_GROUPS, SSM_STATE)
    re_p, im_p = _state_unpack(ht_p)
    re_s, im_s = _state_unpack(ht_s)
    unpack_mem = lambda a: jnp.transpose(a.reshape(1, n_p, CROSS_WIDTH // CROSS_HEAD_DIM, CROSS_HEAD_DIM, N_MEM),
                                         (0, 1, 4, 2, 3))
    k_sample = k_s.reshape(1, n_s, t_s, N_HEADS, 2, HEAD_DIM)
    v_sample = vlin_s.reshape(1, n_s, t_s, N_HEADS, 2 * HEAD_DIM)
    return (y_prompt, y_sample, k_prompt, v_prompt, state4(re_p, n_p), state4(im_p, n_p),
            unpack_mem(mkt), unpack_mem(mvt), k_sample, v_sample, state4(re_s, n_s), state4(im_s, n_s))
```

```python
import functools
import math

import jax
import jax.numpy as jnp
from jax import lax
from jax.experimental import pallas as pl
from jax.experimental.pallas import tpu as pltpu

F32 = jnp.float32
BF16 = jnp.bfloat16

D_MODEL = 1024
HEAD_DIM = 64
N_HEADS = 4
ATTN_WIDTH = 512
ROT_DIM = 16
ROPE_THETA = 500000.0
SSM_WIDTH = 512
SSM_GROUP = 16
SSM_GROUPS = 32
SSM_STATE = 64
N_STATE = SSM_GROUPS * SSM_STATE
PAGE_SIZE = 128
N_MEM = 256
CROSS_WIDTH = 256
CROSS_HEAD_DIM = 64
D_FF = 4096
EPS = 1e-6
NEG_INF = -1e30
LAM_INIT = 0.8 - 0.6 * math.exp(-0.3 * 0)

LANES = 128
SUBLANES = 8
VMEM_LIMIT = 56 * 1024 * 1024

TOK_TILE = 512
PAGES_PER_CHUNK = 16


def _params(sem):
    return pltpu.CompilerParams(dimension_semantics=sem, vmem_limit_bytes=VMEM_LIMIT)


def _rms(x, g):
    ms = jnp.mean(x * x, axis=-1, keepdims=True)
    return x * lax.rsqrt(ms + EPS) * g


def _const_spec(shape):
    nd = len(shape)
    return pl.BlockSpec(shape, lambda *_: (0,) * nd)


def _rope(x, c, s1, s2):
    outs = []
    for i in range(x.shape[1] // LANES):
        xc = x[:, i * LANES:(i + 1) * LANES]
        outs.append(xc * c + pltpu.roll(xc, LANES - ROT_DIM // 2, 1) * s1 + pltpu.roll(xc, ROT_DIM // 2, 1) * s2)
    return jnp.concatenate(outs, axis=1)


def _proj_kernel(x_ref, g_ref, w_ref, c_ref, s1_ref, s2_ref, *out_refs, sample):
    h = _rms(x_ref[...], g_ref[...]).astype(BF16)
    proj = jnp.dot(h, w_ref[...], preferred_element_type=F32)
    c, s1, s2 = c_ref[...], s1_ref[...], s2_ref[...]
    q = _rope(proj[:, :ATTN_WIDTH], c, s1, s2) * (1.0 / math.sqrt(HEAD_DIM))
    k = _rope(proj[:, ATTN_WIDTH:2 * ATTN_WIDTH], c, s1, s2)
    v = proj[:, 2 * ATTN_WIDTH:3 * ATTN_WIDTH]
    if sample:
        q_ref, k_ref, v_ref, vlin_ref, u_ref = out_refs
        q_ref[...] = q
        k_ref[...] = k
        v_ref[...] = v
    else:
        qt_ref, kb_ref, kt_ref, vt_ref, vlin_ref, u_ref = out_refs
        qt_ref[0] = q.T.astype(BF16)
        kb_ref[...] = k.astype(BF16)
        kt_ref[0] = k.T
        vt_ref[0] = v.T.astype(BF16)
    for hh in range(N_HEADS):
        vlin_ref[pl.ds(hh, v.shape[0], stride=N_HEADS), :] = v[:, hh * LANES:(hh + 1) * LANES]
    u_ref[...] = proj[:, 3 * ATTN_WIDTH:]


def _project(x2d, g, w_bf, tabs, n_tab_tiles, rows_per_group, sample):
    m = x2d.shape[0]
    t = TOK_TILE
    groups = m // rows_per_group
    tiles_per_group = rows_per_group // t
    c, s1, s2 = tabs
    tab_spec = pl.BlockSpec((t, LANES), lambda i: (i % n_tab_tiles, 0))
    ft_spec = pl.BlockSpec((1, ATTN_WIDTH, t), lambda i: (i // tiles_per_group, 0, i % tiles_per_group))
    row_spec = pl.BlockSpec((t, ATTN_WIDTH), lambda i: (i, 0))
    lin_spec = pl.BlockSpec((t * N_HEADS, LANES), lambda i: (i, 0))
    row = lambda dt: jax.ShapeDtypeStruct((m, ATTN_WIDTH), dt)
    ft = lambda dt: jax.ShapeDtypeStruct((groups, ATTN_WIDTH, rows_per_group), dt)
    lin = jax.ShapeDtypeStruct((m * N_HEADS, LANES), F32)
    if sample:
        out_specs = [row_spec, row_spec, row_spec, lin_spec, row_spec]
        out_shape = [row(F32), row(F32), row(F32), lin, row(F32)]
    else:
        out_specs = [ft_spec, row_spec, ft_spec, ft_spec, lin_spec, row_spec]
        out_shape = [ft(BF16), row(BF16), ft(F32), ft(BF16), lin, row(F32)]
    return pl.pallas_call(
        functools.partial(_proj_kernel, sample=sample),
        grid=(m // t,),
        in_specs=[pl.BlockSpec((t, D_MODEL), lambda i: (i, 0)),
                  _const_spec((1, D_MODEL)),
                  _const_spec((D_MODEL, 4 * ATTN_WIDTH)),
                  tab_spec, tab_spec, tab_spec],
        out_specs=out_specs,
        out_shape=out_shape,
        compiler_params=_params(("arbitrary",)),
        name="proj",
    )(x2d, g, w_bf, c, s1, s2)


def _rope_tables(pos):
    half = ROT_DIM // 2
    inv_freq = jnp.float32(ROPE_THETA) ** (-jnp.arange(half, dtype=F32) * 2.0 / ROT_DIM)
    ang = pos.astype(F32)[:, None] * inv_freq[None, :]
    cos, sin = jnp.cos(ang), jnp.sin(ang)
    n = pos.shape[0]
    pad = jnp.zeros((n, HEAD_DIM - ROT_DIM), F32)
    c = jnp.concatenate([cos, cos, pad + 1.0], axis=1)
    s1 = jnp.concatenate([-sin, jnp.zeros_like(sin), pad], axis=1)
    s2 = jnp.concatenate([jnp.zeros_like(sin), sin, pad], axis=1)
    tile2 = lambda a: jnp.concatenate([a, a], axis=1)
    return tile2(c), tile2(s1), tile2(s2)


def _lam(lq1, lk1, lq2, lk2):
    return (jnp.exp(jnp.sum(lq1 * lk1, keepdims=True)) - jnp.exp(jnp.sum(lq2 * lk2, keepdims=True))
            + LAM_INIT)


def _subln(o, gain):
    ms = jnp.mean(o * o, axis=-1, keepdims=True)
    return o * lax.rsqrt(ms + EPS) * gain * (1.0 - LAM_INIT)


ATTN_COLS = 256


def _prompt_attn_kernel(qt_ref, k_ref, vt_ref, lq1, lk1, lq2, lk2, gain_ref, o_ref,
                        q2_s, m_s, l_s, acc_s, *, tq):
    qi = pl.program_id(2)
    qt = qt_ref[0]
    feat = lax.broadcasted_iota(jnp.int32, qt.shape, 0)
    zero = jnp.zeros_like(qt)
    q2_s[:, 0:tq] = jnp.where(feat < HEAD_DIM, qt, zero)
    q2_s[:, tq:2 * tq] = jnp.where(feat >= HEAD_DIM, qt, zero)
    m_s[...] = jnp.full(m_s.shape, NEG_INF, F32)
    l_s[...] = jnp.zeros(l_s.shape, F32)
    acc_s[...] = jnp.zeros(acc_s.shape, F32)

    def step(ki, masked):
        start = pl.multiple_of(ki * tq, tq)
        k = k_ref[pl.ds(start, tq), :]
        vt = vt_ref[0, :, pl.ds(start, tq)]
        groups = [slice(g * ATTN_COLS, (g + 1) * ATTN_COLS) for g in range(2 * tq // ATTN_COLS)]
        n_keys = [(cols.stop - 1) % tq + 1 if masked else tq for cols in groups]
        scores = [jnp.dot(k[:nk], q2_s[:, cols], preferred_element_type=F32) for cols, nk in zip(groups, n_keys)]
        probs, stats = [], []
        for cols, s in zip(groups, scores):
            if masked:
                key = lax.broadcasted_iota(jnp.int32, s.shape, 0)
                qry = (lax.broadcasted_iota(jnp.int32, s.shape, 1) + cols.start) % tq
                s = jnp.where(key <= qry, s, NEG_INF)
            m_old = m_s[:, cols]
            m_new = jnp.maximum(m_old, jnp.max(s, axis=0, keepdims=True))
            alpha = jnp.exp(m_old - m_new)
            p = jnp.exp(s - m_new)
            probs.append(p.astype(BF16))
            stats.append((m_new, alpha, alpha * l_s[:, cols] + jnp.sum(p, axis=0, keepdims=True)))
        for cols, nk, p, (m_new, alpha, l_new) in zip(groups, n_keys, probs, stats):
            acc_s[:, cols] = alpha * acc_s[:, cols] + jnp.dot(vt[:, :nk], p, preferred_element_type=F32)
            m_s[:, cols] = m_new
            l_s[:, cols] = l_new

    def body(ki, carry):
        step(ki, False)
        return carry

    lax.fori_loop(0, qi, body, 0)
    step(qi, True)

    lam = _lam(lq1[...], lk1[...], lq2[...], lk2[...])
    o = acc_s[...] / l_s[...]
    o = o[:, 0:tq] - lam * o[:, tq:2 * tq]
    ms = jnp.mean(o * o, axis=0, keepdims=True)
    o = o * lax.rsqrt(ms + EPS) * gain_ref[...] * (1.0 - LAM_INIT)
    o_ref[...] = o.T.astype(o_ref.dtype)


def _prompt_attention(qt_bf, k_bf, vt_bf, lam_params, gain_col, n_batch, seq, tq=512):
    nq = seq // tq
    lam_spec = _const_spec((1, HEAD_DIM))
    return pl.pallas_call(
        functools.partial(_prompt_attn_kernel, tq=tq),
        grid=(n_batch, N_HEADS, nq),
        in_specs=[pl.BlockSpec((1, LANES, tq), lambda n, h, i: (n, h, i)),
                  pl.BlockSpec((seq, LANES), lambda n, h, i: (n, h)),
                  pl.BlockSpec((1, LANES, seq), lambda n, h, i: (n, h, 0)),
                  lam_spec, lam_spec, lam_spec, lam_spec,
                  _const_spec((LANES, 1))],
        out_specs=pl.BlockSpec((tq, LANES), lambda n, h, i: (n * nq + i, h)),
        out_shape=jax.ShapeDtypeStruct((n_batch * seq, ATTN_WIDTH), BF16),
        scratch_shapes=[pltpu.VMEM((LANES, 2 * tq), BF16),
                        pltpu.VMEM((1, 2 * tq), F32),
                        pltpu.VMEM((1, 2 * tq), F32),
                        pltpu.VMEM((LANES, 2 * tq), F32)],
        compiler_params=_params(("arbitrary", "arbitrary", "arbitrary")),
        name="prompt_attn",
    )(qt_bf, k_bf, vt_bf, *lam_params, gain_col)


SEQS_PER_STEP = 2


def _paged_mlp_kernel(pt_ref, q_ref, knew_ref, vnew_ref, lq1, lk1, lq2, lk2, gain_ref,
                      x_ref, g_ref, gf_ref, ck_hbm, cv_hbm, wu_hbm, wd_hbm,
                      o_ref, y_ref, kbuf, vbuf, sem, wu_s, wd_s, wsem, h_s, acc_s,
                      *, n_steps, n_chunks, t_new):
    g_idx = pl.program_id(0)
    ppc = PAGES_PER_CHUNK
    spg = SEQS_PER_STEP
    rows = 2 * N_HEADS * t_new
    n_iter = spg * n_chunks
    ff_chunk = D_FF // n_iter

    def page_copies(seq, chunk, slot):
        cps = []
        for pg in range(ppc):
            page = pt_ref[chunk * ppc + pg, seq]
            cps.append(pltpu.make_async_copy(ck_hbm.at[page], kbuf.at[slot, pg], sem.at[0, slot]))
            cps.append(pltpu.make_async_copy(cv_hbm.at[page], vbuf.at[slot, pg], sem.at[1, slot]))
        return cps

    def weight_copies():
        return [pltpu.make_async_copy(wu_hbm, wu_s, wsem.at[0]), pltpu.make_async_copy(wd_hbm, wd_s, wsem.at[1])]

    @pl.when(g_idx == 0)
    def _():
        for cp in page_copies(0, 0, 0):
            cp.start()
        for cp in weight_copies():
            cp.start()
        for cp in weight_copies():
            cp.wait()

    def mlp_part(j):
        if j == 0:
            x = x_ref[...]
            h_s[...] = _rms(x, g_ref[...]).astype(BF16)
            acc_s[...] = x
        z = jnp.dot(h_s[...], wu_s[:, j * ff_chunk:(j + 1) * ff_chunk], preferred_element_type=F32)
        a = jnp.square(jnp.maximum(z, 0.0)).astype(BF16)
        acc_s[...] += jnp.dot(a, wd_s[j * ff_chunk:(j + 1) * ff_chunk, :], preferred_element_type=F32)
        if j == n_iter - 1:
            y_ref[...] = _rms(acc_s[...], gf_ref[...])

    lam = _lam(lq1[...], lk1[...], lq2[...], lk2[...])
    gain = gain_ref[...]

    for j in range(n_iter):
        i, c = divmod(j, n_chunks)
        seq = g_idx * spg + i
        slot = j % 2
        nslot = (j + 1) % 2
        if j + 1 < n_iter:
            i2, c2 = divmod(j + 1, n_chunks)
            for cp in page_copies(g_idx * spg + i2, c2, nslot):
                cp.start()
        else:
            @pl.when(g_idx + 1 < n_steps)
            def _():
                for cp in page_copies((g_idx + 1) * spg, 0, nslot):
                    cp.start()
        mlp_part(j)
        for cp in page_copies(seq, c, slot):
            cp.wait()
        if c == 0:
            q = q_ref[i * t_new:(i + 1) * t_new, :]
            qt = jnp.concatenate([q] * (2 * N_HEADS), axis=0)
            r_hj = lax.broadcasted_iota(jnp.int32, qt.shape, 0) // t_new
            c_hj = lax.broadcasted_iota(jnp.int32, qt.shape, 1) // HEAD_DIM
            qbd = jnp.where(r_hj == c_hj, qt, 0.0).astype(BF16)
            m = jnp.full((rows, 1), NEG_INF, F32)
            l = jnp.zeros((rows, 1), F32)
            acc = jnp.zeros((rows, ATTN_WIDTH), F32)
        kt = jnp.concatenate([kbuf[slot, pg].astype(BF16) for pg in range(ppc)], axis=1)
        s = jnp.dot(qbd, kt, preferred_element_type=F32)
        m_new = jnp.maximum(m, jnp.max(s, axis=-1, keepdims=True))
        alpha = jnp.exp(m - m_new)
        p = jnp.exp(s - m_new)
        l = alpha * l + jnp.sum(p, axis=-1, keepdims=True)
        v = jnp.concatenate(
            [jnp.concatenate([vbuf[slot, pg, pl.ds(hh, PAGE_SIZE, stride=N_HEADS), :] for hh in range(N_HEADS)],
                             axis=1) for pg in range(ppc)], axis=0).astype(BF16)
        acc = alpha * acc + jnp.dot(p.astype(BF16), v, preferred_element_type=F32)
        m = m_new
        if c < n_chunks - 1:
            continue

        pad = jnp.zeros((t_new, ATTN_WIDTH), F32)
        knew = jnp.concatenate([knew_ref[i * t_new:(i + 1) * t_new, :], pad], axis=0).astype(BF16)
        vnew = jnp.concatenate([vnew_ref[i * t_new:(i + 1) * t_new, :], pad], axis=0).astype(BF16)
        s = lax.dot_general(qbd, knew, (((1,), (1,)), ((), ())), preferred_element_type=F32)
        row_t = lax.broadcasted_iota(jnp.int32, s.shape, 0) % t_new
        col_t = lax.broadcasted_iota(jnp.int32, s.shape, 1)
        s = jnp.where(col_t <= row_t, s, NEG_INF)
        m_new = jnp.maximum(m, jnp.max(s, axis=-1, keepdims=True))
        alpha = jnp.exp(m - m_new)
        p = jnp.exp(s - m_new)
        l = alpha * l + jnp.sum(p, axis=-1, keepdims=True)
        acc = alpha * acc + jnp.dot(p.astype(BF16), vnew, preferred_element_type=F32)
        o_all = acc / l
        outs = []
        for hh in range(N_HEADS):
            r0 = hh * 2 * t_new
            blk = o_all[r0:r0 + 2 * t_new, hh * LANES:(hh + 1) * LANES]
            o = blk[0:t_new, :] - lam * blk[t_new:2 * t_new, :]
            outs.append(_subln(o, gain))
        o_ref[i * t_new:(i + 1) * t_new, :] = jnp.concatenate(outs, axis=1)


def _paged_mlp(pt_t, q, knew, vnew, lam_params, gain, ck, cv, x2d, g, gf, wu, wd, n_seq, t_new):
    n_pages = pt_t.shape[0]
    n_chunks = n_pages // PAGES_PER_CHUNK
    n_steps = n_seq // SEQS_PER_STEP
    m = x2d.shape[0]
    t = m // n_steps
    assert n_seq % SEQS_PER_STEP == 0 and m % n_steps == 0 and t % SUBLANES == 0
    assert D_FF % (SEQS_PER_STEP * n_chunks) == 0 and (SEQS_PER_STEP * n_chunks) % 2 == 0
    cmap = lambda s, pt: (0, 0)
    lam_spec = pl.BlockSpec((1, HEAD_DIM), cmap)
    row_spec = pl.BlockSpec((SEQS_PER_STEP * t_new, ATTN_WIDTH), lambda s, pt: (s, 0))
    x_spec = pl.BlockSpec((t, D_MODEL), lambda s, pt: (s, 0))
    vec_spec = pl.BlockSpec((1, D_MODEL), cmap)
    any_spec = pl.BlockSpec(memory_space=pl.ANY)
    feat = ck.shape[1]
    grid_spec = pltpu.PrefetchScalarGridSpec(
        num_scalar_prefetch=1,
        grid=(n_steps,),
        in_specs=[row_spec, row_spec, row_spec,
                  lam_spec, lam_spec, lam_spec, lam_spec,
                  pl.BlockSpec((1, LANES), cmap),
                  x_spec, vec_spec, vec_spec,
                  any_spec, any_spec, any_spec, any_spec],
        out_specs=[row_spec, x_spec],
        scratch_shapes=[pltpu.VMEM((2, PAGES_PER_CHUNK, feat, PAGE_SIZE), F32),
                        pltpu.VMEM((2, PAGES_PER_CHUNK, PAGE_SIZE * N_HEADS, LANES), F32),
                        pltpu.SemaphoreType.DMA((2, 2)),
                        pltpu.VMEM((D_MODEL, D_FF), BF16),
                        pltpu.VMEM((D_FF, D_MODEL), BF16),
                        pltpu.SemaphoreType.DMA((2,)),
                        pltpu.VMEM((t, D_MODEL), BF16),
                        pltpu.VMEM((t, D_MODEL), F32)],
    )
    return pl.pallas_call(
        functools.partial(_paged_mlp_kernel, n_steps=n_steps, n_chunks=n_chunks, t_new=t_new),
        grid_spec=grid_spec,
        out_shape=[jax.ShapeDtypeStruct((n_seq * t_new, ATTN_WIDTH), F32),
                   jax.ShapeDtypeStruct((m, D_MODEL), F32)],
        compiler_params=_params(("arbitrary",)),
        name="paged_mlp",
    )(pt_t, q, knew, vnew, *lam_params, gain, x2d, g, gf, ck, cv, wu, wd)


HALF_STATE = N_STATE // 2
S5_TIME_TILE = 32


def _s5_tail(x_bf, u, cc_ref, d_ref, wg_ref, bg_ref):
    y = jnp.concatenate(
        [jnp.dot(x_bf[:, hf * N_STATE:(hf + 1) * N_STATE], cc_ref[hf], preferred_element_type=F32) for hf in range(2)],
        axis=1) + d_ref[...] * u
    g = 0.5 * y * (1.0 + lax.erf(y * (1.0 / math.sqrt(2.0))))
    z = jnp.dot(g.astype(BF16), wg_ref[...], preferred_element_type=F32) + bg_ref[...]
    return g * (1.0 / (1.0 + jnp.exp(-z)))


def _s5_bu(u_bf, bb_ref, bu_s):
    half_u = SSM_WIDTH // 2
    for hf in range(2):
        bu_s[:, hf * N_STATE:(hf + 1) * N_STATE] = jnp.dot(u_bf[:, hf * half_u:(hf + 1) * half_u], bb_ref[hf],
                                                           preferred_element_type=F32)


def _s5_prompt_kernel(u_ref, perm_ref, permt_ref, bb_ref, cc_ref, lb_ref, d_ref, wg_ref, bg_ref,
                      ys_ref, ht_ref, bu_s, carry_s):
    nb, tt = u_ref.shape[0], u_ref.shape[1]
    rows = nb * tt

    @pl.when(pl.program_id(0) == 0)
    def _():
        carry_s[...] = jnp.zeros(carry_s.shape, F32)

    u = u_ref[...].reshape(rows, SSM_WIDTH)
    hi = u.astype(BF16)
    r1 = u - hi.astype(F32)
    mid = r1.astype(BF16)
    lo = (r1 - mid.astype(F32)).astype(BF16)
    perm = perm_ref[...]
    u_hi = jnp.dot(perm, hi, preferred_element_type=F32)
    u_tb = u_hi + jnp.dot(perm, mid, preferred_element_type=F32) + jnp.dot(perm, lo, preferred_element_type=F32)
    _s5_bu(u_hi.astype(BF16), bb_ref, bu_s)

    for hf in range(2):
        c0 = hf * N_STATE
        lbr = lb_ref[0, :, hf * HALF_STATE:(hf + 1) * HALF_STATE]
        lbi = lb_ref[1, :, hf * HALF_STATE:(hf + 1) * HALF_STATE]

        def body(t, carry, c0=c0, lbr=lbr, lbi=lbi):
            xr, xi = carry
            r0 = pl.multiple_of(t * nb, nb)
            br = bu_s[pl.ds(r0, nb), c0:c0 + HALF_STATE]
            bi = bu_s[pl.ds(r0, nb), c0 + HALF_STATE:c0 + N_STATE]
            xr, xi = lbr * xr - lbi * xi + br, lbr * xi + lbi * xr + bi
            bu_s[pl.ds(r0, nb), c0:c0 + HALF_STATE] = xr
            bu_s[pl.ds(r0, nb), c0 + HALF_STATE:c0 + N_STATE] = xi
            return xr, xi

        xr, xi = lax.fori_loop(0, tt, body, (carry_s[:, c0:c0 + HALF_STATE], carry_s[:, c0 + HALF_STATE:c0 + N_STATE]),
                               unroll=4)
        carry_s[:, c0:c0 + HALF_STATE] = xr
        carry_s[:, c0 + HALF_STATE:c0 + N_STATE] = xi
    ht_ref[...] = carry_s[...]

    ys_tb = _s5_tail(bu_s[...].astype(BF16), u_tb, cc_ref, d_ref, wg_ref, bg_ref).astype(BF16)
    ys = jnp.dot(permt_ref[...], ys_tb, preferred_element_type=F32)
    ys_ref[...] = ys.astype(ys_ref.dtype).reshape(nb, tt, SSM_WIDTH)


def _s5_prompt(u3, ssm):
    bb, cc, lb_tab, a_tab, pw_tab, d_row, wg, bg = ssm
    nb, seq = u3.shape[0], u3.shape[1]
    tt = S5_TIME_TILE
    rows = nb * tt
    r = jnp.arange(rows)
    perm = jax.nn.one_hot((r % nb) * tt + r // nb, rows, dtype=BF16)
    blk = pl.BlockSpec((nb, tt, SSM_WIDTH), lambda i: (0, i, 0))
    return pl.pallas_call(
        _s5_prompt_kernel,
        grid=(seq // tt,),
        in_specs=[blk, _const_spec((rows, rows)), _const_spec((rows, rows)),
                  _const_spec(bb.shape), _const_spec(cc.shape), _const_spec(lb_tab.shape),
                  _const_spec(d_row.shape), _const_spec(wg.shape), _const_spec(bg.shape)],
        out_specs=[blk, _const_spec((nb, 2 * N_STATE))],
        out_shape=[jax.ShapeDtypeStruct((nb, seq, SSM_WIDTH), BF16),
                   jax.ShapeDtypeStruct((nb, 2 * N_STATE), F32)],
        scratch_shapes=[pltpu.VMEM((rows, 2 * N_STATE), F32), pltpu.VMEM((nb, 2 * N_STATE), F32)],
        compiler_params=_params(("arbitrary",)),
        name="s5_prompt",
    )(u3, perm, perm.T, bb, cc, lb_tab, d_row, wg, bg)


def _s5_sample_kernel(u_ref, h0_ref, bb_ref, cc_ref, a_ref, pw_ref, d_ref, wg_ref, bg_ref,
                      ys_ref, ht_ref, bu_s):
    t = u_ref.shape[0]
    u = u_ref[...]
    _s5_bu(u.astype(BF16), bb_ref, bu_s)

    def cmul_add(xr, xi, ar, ai, sr, si):
        return xr + ar * sr - ai * si, xi + ar * si + ai * sr

    def body(b, carry):
        r0 = pl.multiple_of(b * SUBLANES, SUBLANES)
        init = h0_ref[pl.ds(b, 1), :]
        for hf in range(2):
            c0 = hf * N_STATE
            st = slice(hf * HALF_STATE, (hf + 1) * HALF_STATE)
            xr = bu_s[pl.ds(r0, SUBLANES), c0:c0 + HALF_STATE]
            xi = bu_s[pl.ds(r0, SUBLANES), c0 + HALF_STATE:c0 + N_STATE]
            for k, shift in enumerate((1, 2, 4)):
                sr = pltpu.roll(xr, shift, 0)
                si = pltpu.roll(xi, shift, 0)
                xr, xi = cmul_add(xr, xi, a_ref[k, 0, :, st], a_ref[k, 1, :, st], sr, si)
            cr = jnp.broadcast_to(init[:, c0:c0 + HALF_STATE], xr.shape)
            ci = jnp.broadcast_to(init[:, c0 + HALF_STATE:c0 + N_STATE], xi.shape)
            xr, xi = cmul_add(xr, xi, pw_ref[0, :, st], pw_ref[1, :, st], cr, ci)
            bu_s[pl.ds(r0, SUBLANES), c0:c0 + HALF_STATE] = xr
            bu_s[pl.ds(r0, SUBLANES), c0 + HALF_STATE:c0 + N_STATE] = xi
            ht_ref[pl.ds(b, 1), c0:c0 + HALF_STATE] = xr[SUBLANES - 1:SUBLANES, :]
            ht_ref[pl.ds(b, 1), c0 + HALF_STATE:c0 + N_STATE] = xi[SUBLANES - 1:SUBLANES, :]
        return carry

    lax.fori_loop(0, t // SUBLANES, body, 0)
    ys_ref[...] = _s5_tail(bu_s[...].astype(BF16), u, cc_ref, d_ref, wg_ref, bg_ref).astype(ys_ref.dtype)


def _s5_sample(u, h0, ssm):
    bb, cc, lb_tab, a_tab, pw_tab, d_row, wg, bg = ssm
    t = TOK_TILE
    m = u.shape[0]
    h_spec = pl.BlockSpec((t // SUBLANES, 2 * N_STATE), lambda i: (i, 0))
    return pl.pallas_call(
        _s5_sample_kernel,
        grid=(m // t,),
        in_specs=[pl.BlockSpec((t, SSM_WIDTH), lambda i: (i, 0)), h_spec,
                  _const_spec(bb.shape), _const_spec(cc.shape), _const_spec(a_tab.shape), _const_spec(pw_tab.shape),
                  _const_spec(d_row.shape), _const_spec(wg.shape), _const_spec(bg.shape)],
        out_specs=[pl.BlockSpec((t, SSM_WIDTH), lambda i: (i, 0)), h_spec],
        out_shape=[jax.ShapeDtypeStruct((m, SSM_WIDTH), BF16),
                   jax.ShapeDtypeStruct((m // SUBLANES, 2 * N_STATE), F32)],
        scratch_shapes=[pltpu.VMEM((t, 2 * N_STATE), F32)],
        compiler_params=_params(("arbitrary",)),
        name="s5_sample",
    )(u, h0, bb, cc, a_tab, pw_tab, d_row, wg, bg)


def _state_pack(re, im):
    return jnp.concatenate([re[:, :HALF_STATE], im[:, :HALF_STATE], re[:, HALF_STATE:], im[:, HALF_STATE:]], axis=1)


def _state_unpack(x):
    re = jnp.concatenate([x[:, 0:HALF_STATE], x[:, N_STATE:N_STATE + HALF_STATE]], axis=1)
    im = jnp.concatenate([x[:, HALF_STATE:N_STATE], x[:, N_STATE + HALF_STATE:]], axis=1)
    return re, im


def _ssm_tables(a_re, a_im, log_dt, b_re, b_im, c_re, c_im, d_skip, w_glu, b_glu):
    delta = jnp.exp(log_dt)[:, None]
    mag = jnp.exp(a_re * delta)
    ang = a_im * delta
    lb_re = mag * jnp.cos(ang)
    lb_im = mag * jnp.sin(ang)
    den = a_re * a_re + a_im * a_im
    num_re = lb_re - 1.0
    cz_re = (num_re * a_re + lb_im * a_im) / den
    cz_im = (lb_im * a_re - num_re * a_im) / den
    bb_re = cz_re[..., None] * b_re - cz_im[..., None] * b_im
    bb_im = cz_re[..., None] * b_im + cz_im[..., None] * b_re
    gh = SSM_GROUPS // 2
    eye = jnp.eye(gh, dtype=F32)
    bd_in = lambda w: jnp.einsum('gsp,gh->gphs', w, eye).reshape(gh * SSM_GROUP, HALF_STATE)
    bd_out = lambda w: jnp.einsum('gps,gh->gshp', w, eye).reshape(HALF_STATE, gh * SSM_GROUP)
    halves = lambda w: (w[:gh], w[gh:])
    bb = jnp.stack([jnp.concatenate([bd_in(r), bd_in(i)], axis=1)
                    for r, i in zip(halves(bb_re), halves(bb_im))]).astype(BF16)
    cc = jnp.stack([jnp.concatenate([bd_out(r), bd_out(-i)], axis=0)
                    for r, i in zip(halves(c_re), halves(c_im))]).astype(BF16)

    def cmul(ar, ai, br, bi):
        return ar * br - ai * bi, ar * bi + ai * br

    l1 = (lb_re.reshape(N_STATE), lb_im.reshape(N_STATE))
    pows = [l1]
    for _ in range(SUBLANES - 1):
        pows.append(cmul(*pows[-1], *l1))
    rows = jnp.arange(SUBLANES)[:, None]
    a_tab = jnp.stack([jnp.stack([jnp.where(rows >= sh, pows[sh - 1][0][None, :], 0.0),
                                  jnp.where(rows >= sh, pows[sh - 1][1][None, :], 0.0)]) for sh in (1, 2, 4)])
    pw_tab = jnp.stack([jnp.stack([p[0] for p in pows]), jnp.stack([p[1] for p in pows])])
    lb_tab = jnp.stack([jnp.broadcast_to(l1[0][None, :], (SUBLANES, N_STATE)),
                        jnp.broadcast_to(l1[1][None, :], (SUBLANES, N_STATE))])
    return (bb, cc, lb_tab, a_tab.astype(F32), pw_tab, d_skip.reshape(1, SSM_WIDTH),
            w_glu.astype(BF16), b_glu.reshape(1, SSM_WIDTH))


def _memkv_kernel(mem_ref, g_ref, wk_ref, wv_ref, kt_ref, vt_ref, ktb_ref, vtb_ref):
    mn = _rms(mem_ref[...], g_ref[...]).astype(BF16)
    kt = jnp.dot(mn, wk_ref[...], preferred_element_type=F32).T
    vt = jnp.dot(mn, wv_ref[...], preferred_element_type=F32).T
    kt_ref[0] = kt
    vt_ref[0] = vt
    ktb_ref[0] = kt.astype(BF16)
    vtb_ref[0] = vt.astype(BF16)


def _memory_kv(mem2d, g, wk, wv, n_batch):
    spec = pl.BlockSpec((1, CROSS_WIDTH, N_MEM), lambda n: (n, 0, 0))
    f = jax.ShapeDtypeStruct((n_batch, CROSS_WIDTH, N_MEM), F32)
    b = jax.ShapeDtypeStruct((n_batch, CROSS_WIDTH, N_MEM), BF16)
    return pl.pallas_call(
        _memkv_kernel,
        grid=(n_batch,),
        in_specs=[pl.BlockSpec((N_MEM, D_MODEL), lambda n: (n, 0)), _const_spec((1, D_MODEL)),
                  _const_spec((D_MODEL, CROSS_WIDTH)), _const_spec((D_MODEL, CROSS_WIDTH))],
        out_specs=[spec, spec, spec, spec],
        out_shape=[f, f, b, b],
        compiler_params=_params(("arbitrary",)),
        name="memory_kv",
    )(mem2d, g, wk, wv)


def _cross_attend(hq, kt, vt, n_q):
    heads = CROSS_WIDTH // CROSS_HEAD_DIM
    stacked = jnp.concatenate([hq] * heads, axis=0)
    r_h = lax.broadcasted_iota(jnp.int32, stacked.shape, 0) // n_q
    c_h = lax.broadcasted_iota(jnp.int32, stacked.shape, 1) // CROSS_HEAD_DIM
    own = r_h == c_h
    s = jnp.dot(jnp.where(own, stacked, 0.0).astype(BF16), kt, preferred_element_type=F32)
    p = jnp.exp(s - jnp.max(s, axis=-1, keepdims=True))
    p = p / jnp.sum(p, axis=-1, keepdims=True)
    full = lax.dot_general(p.astype(BF16), vt, (((1,), (1,)), ((), ())), preferred_element_type=F32)
    full = jnp.where(own, full, 0.0)
    out = full[0:n_q, :]
    for hh in range(1, heads):
        out = out + full[hh * n_q:(hh + 1) * n_q, :]
    return out


def _mix_cross_kernel(x_ref, o_ref, ys_ref, wo_ref, g_ref, wq_ref, kt_ref, vt_ref, wco_ref, out_ref, *, seq_rows):
    o = o_ref[...].astype(BF16)
    mix = (jnp.dot(o, wo_ref[0:ATTN_WIDTH, :], preferred_element_type=F32)
           + jnp.dot(ys_ref[...], wo_ref[ATTN_WIDTH:, :], preferred_element_type=F32))
    x1 = x_ref[...] + mix
    hq = jnp.dot(_rms(x1, g_ref[...]).astype(BF16), wq_ref[...], preferred_element_type=F32)
    hq = hq * (1.0 / math.sqrt(CROSS_HEAD_DIM))
    pieces = []
    for j in range(x1.shape[0] // seq_rows):
        pieces.append(_cross_attend(hq[j * seq_rows:(j + 1) * seq_rows, :], kt_ref[j].astype(BF16),
                                    vt_ref[j].astype(BF16), seq_rows))
    oc = jnp.concatenate(pieces, axis=0)
    out_ref[...] = x1 + jnp.dot(oc.astype(BF16), wco_ref[...], preferred_element_type=F32)


def _mix_cross(x2d, o, ys, wo, g, wq, kt, vt, wco, seq_rows, t, tiles_per_mem):
    m = x2d.shape[0]
    mem_spec = pl.BlockSpec((t // seq_rows, CROSS_WIDTH, N_MEM), lambda i: (i // tiles_per_mem, 0, 0))
    return pl.pallas_call(
        functools.partial(_mix_cross_kernel, seq_rows=seq_rows),
        grid=(m // t,),
        in_specs=[pl.BlockSpec((t, D_MODEL), lambda i: (i, 0)),
                  pl.BlockSpec((t, ATTN_WIDTH), lambda i: (i, 0)),
                  pl.BlockSpec((t, SSM_WIDTH), lambda i: (i, 0)),
                  _const_spec((D_MODEL, D_MODEL)), _const_spec((1, D_MODEL)),
                  _const_spec((D_MODEL, CROSS_WIDTH)), mem_spec, mem_spec,
                  _const_spec((CROSS_WIDTH, D_MODEL))],
        out_specs=pl.BlockSpec((t, D_MODEL), lambda i: (i, 0)),
        out_shape=jax.ShapeDtypeStruct((m, D_MODEL), F32),
        compiler_params=_params(("arbitrary",)),
        name="mix_cross",
    )(x2d, o, ys, wo, g, wq, kt, vt, wco)


def _mlp_kernel(x_ref, g_ref, wu_ref, wd_ref, gf_ref, y_ref, *, ff_chunk):
    x = x_ref[...]
    h = _rms(x, g_ref[...]).astype(BF16)
    acc = x
    for c in range(D_FF // ff_chunk):
        z = jnp.dot(h, wu_ref[:, c * ff_chunk:(c + 1) * ff_chunk], preferred_element_type=F32)
        a = jnp.square(jnp.maximum(z, 0.0)).astype(BF16)
        acc = acc + jnp.dot(a, wd_ref[c * ff_chunk:(c + 1) * ff_chunk, :], preferred_element_type=F32)
    y_ref[...] = _rms(acc, gf_ref[...])


def _mlp(x2d, g, wu, wd, gf, ff_chunk=1024):
    m = x2d.shape[0]
    t = TOK_TILE
    return pl.pallas_call(
        functools.partial(_mlp_kernel, ff_chunk=ff_chunk),
        grid=(m // t,),
        in_specs=[pl.BlockSpec((t, D_MODEL), lambda i: (i, 0)), _const_spec((1, D_MODEL)),
                  _const_spec((D_MODEL, D_FF)), _const_spec((D_FF, D_MODEL)), _const_spec((1, D_MODEL))],
        out_specs=pl.BlockSpec((t, D_MODEL), lambda i: (i, 0)),
        out_shape=jax.ShapeDtypeStruct((m, D_MODEL), F32),
        compiler_params=_params(("arbitrary",)),
        name="mlp",
    )(x2d, g, wu, wd, gf)


def kernel(x_prompt, x_sample, mem_prompt, cache_k, cache_v, page_table, state_ssm_re, state_ssm_im, cache_mem_k, cache_mem_v, norm_mix, w_in, lambda_q1, lambda_k1, lambda_q2, lambda_k2, subln_gain, ssm_a_re, ssm_a_im, ssm_log_dt, ssm_b_re, ssm_b_im, ssm_c_re, ssm_c_im, ssm_d, w_glu, b_glu, w_out, norm_cross, norm_mem, w_cq, w_ck, w_cv, w_co, norm_mlp, w_up, w_down, final_norm):
    n_p, t_p = x_prompt.shape[0], x_prompt.shape[1]
    n_s, t_s = x_sample.shape[0], x_sample.shape[1]
    n_pool = cache_k.shape[1]
    past = page_table.shape[1] * PAGE_SIZE
    assert cache_k.shape[0] == 1 and t_s == SUBLANES and n_p == SUBLANES
    assert t_p % TOK_TILE == 0 and (n_s * t_s) % TOK_TILE == 0

    l = 0
    w_in_b = w_in[l].astype(BF16)
    w_out_b = w_out[l].astype(BF16)
    w_cq_b, w_ck_b, w_cv_b, w_co_b = (w[l].astype(BF16) for w in (w_cq, w_ck, w_cv, w_co))
    w_up_b, w_down_b = w_up[l].astype(BF16), w_down[l].astype(BF16)
    lam_params = (lambda_q1, lambda_k1, lambda_q2, lambda_k2)
    ssm = _ssm_tables(ssm_a_re[l], ssm_a_im[l], ssm_log_dt[l], ssm_b_re[l], ssm_b_im[l],
                      ssm_c_re[l], ssm_c_im[l], ssm_d[l], w_glu[l], b_glu[l])
    final_g = final_norm.reshape(1, D_MODEL)

    xp = x_prompt.reshape(n_p * t_p, D_MODEL)
    tabs_p = _rope_tables(jnp.arange(t_p, dtype=jnp.int32))
    qt_p, kb_p, kt_p, vt_p, vlin_p, u_p = _project(xp, norm_mix, w_in_b, tabs_p, t_p // TOK_TILE, t_p, False)
    o_p = _prompt_attention(qt_p, kb_p, vt_p, lam_params, subln_gain.reshape(LANES, 1), n_p, t_p)
    ys_p, ht_p = _s5_prompt(u_p.reshape(n_p, t_p, SSM_WIDTH), ssm)
    mkt, mvt, mktb, mvtb = _memory_kv(mem_prompt.reshape(n_p * N_MEM, D_MODEL), norm_mem, w_ck_b, w_cv_b, n_p)
    x2_p = _mix_cross(xp, o_p, ys_p.reshape(n_p * t_p, SSM_WIDTH), w_out_b, norm_cross, w_cq_b, mktb, mvtb, w_co_b,
                      TOK_TILE, TOK_TILE, t_p // TOK_TILE)

    xs = x_sample.reshape(n_s * t_s, D_MODEL)
    pos_s = past + (jnp.arange(TOK_TILE, dtype=jnp.int32) % t_s)
    tabs_s = _rope_tables(pos_s)
    q_s, k_s, v_s, vlin_s, u_s = _project(xs, norm_mix, w_in_b, tabs_s, 1, n_s * t_s, True)
    ck = jnp.transpose(cache_k, (0, 1, 3, 4, 5, 2)).reshape(n_pool, ATTN_WIDTH, PAGE_SIZE)
    cv = cache_v.reshape(n_pool, PAGE_SIZE * N_HEADS, LANES)
    o_s, y_p = _paged_mlp(page_table.T, q_s, k_s, v_s, lam_params, subln_gain, ck, cv,
                          x2_p, norm_mlp, final_g, w_up_b, w_down_b, n_s, t_s)
    h0_s = _state_pack(state_ssm_re[l].reshape(n_s, N_STATE), state_ssm_im[l].reshape(n_s, N_STATE))
    ys_s, ht_s = _s5_sample(u_s, h0_s, ssm)
    cmk = jnp.transpose(cache_mem_k[l], (0, 2, 3, 1)).reshape(n_s, CROSS_WIDTH, N_MEM)
    cmv = jnp.transpose(cache_mem_v[l], (0, 2, 3, 1)).reshape(n_s, CROSS_WIDTH, N_MEM)
    x2_s = _mix_cross(xs, o_s, ys_s, w_out_b, norm_cross, w_cq_b, cmk, cmv, w_co_b, t_s, 128, 1)
    y_s = _mlp(x2_s, norm_mlp, w_up_b, w_down_b, final_g)

    y_prompt = y_p.reshape(n_p, t_p, D_MODEL)
    y_sample = y_s.reshape(n_s, t_s, D_MODEL)
    k_prompt = jnp.transpose(kt_p.reshape(1, n_p, N_HEADS, 2, HEAD_DIM, t_p), (0, 1, 5, 2, 3, 4))
    v_prompt = vlin_p.reshape(1, n_p, t_p, N_HEADS, 2 * HEAD_DIM)
    state4 = lambda a, n: a.reshape(1, n, SSM_GROUPS, SSM_STATE)
    re_p, im_p = _state_unpack(ht_p)
    re_s, im_s = _state_unpack(ht_s)
    unpack_mem = lambda a: jnp.transpose(a.reshape(1, n_p, CROSS_WIDTH // CROSS_HEAD_DIM, CROSS_HEAD_DIM, N_MEM),
                                         (0, 1, 4, 2, 3))
    k_sample = k_s.reshape(1, n_s, t_s, N_HEADS, 2, HEAD_DIM)
    v_sample = vlin_s.reshape(1, n_s, t_s, N_HEADS, 2 * HEAD_DIM)
    return (y_prompt, y_sample, k_prompt, v_prompt, state4(re_p, n_p), state4(im_p, n_p),
            unpack_mem(mkt), unpack_mem(mvt), k_sample, v_sample, state4(re_s, n_s), state4(im_s, n_s))
```

```python
import functools
import math

import jax
import jax.numpy as jnp
from jax import lax
from jax.experimental import pallas as pl
from jax.experimental.pallas import tpu as pltpu

F32 = jnp.float32
BF16 = jnp.bfloat16

D_MODEL = 1024
HEAD_DIM = 64
N_HEADS = 4
ATTN_WIDTH = 512
ROT_DIM = 16
ROPE_THETA = 500000.0
SSM_WIDTH = 512
SSM_GROUP = 16
SSM_GROUPS = 32
SSM_STATE = 64
N_STATE = SSM_GROUPS * SSM_STATE
PAGE_SIZE = 128
N_MEM = 256
CROSS_WIDTH = 256
CROSS_HEAD_DIM = 64
D_FF = 4096
EPS = 1e-6
NEG_INF = -1e30
LAM_INIT = 0.8 - 0.6 * math.exp(-0.3 * 0)

LANES = 128
SUBLANES = 8
VMEM_LIMIT = 56 * 1024 * 1024

TOK_TILE = 1024
PAGES_PER_CHUNK = 16


def _params(sem):
    return pltpu.CompilerParams(dimension_semantics=sem, vmem_limit_bytes=VMEM_LIMIT)


def _rms(x, g):
    ms = jnp.mean(x * x, axis=-1, keepdims=True)
    return x * lax.rsqrt(ms + EPS) * g


def _const_spec(shape):
    nd = len(shape)
    return pl.BlockSpec(shape, lambda *_: (0,) * nd)


def _rope(x, c, s1, s2):
    outs = []
    for i in range(x.shape[1] // LANES):
        xc = x[:, i * LANES:(i + 1) * LANES]
        outs.append(xc * c + pltpu.roll(xc, LANES - ROT_DIM // 2, 1) * s1 + pltpu.roll(xc, ROT_DIM // 2, 1) * s2)
    return jnp.concatenate(outs, axis=1)


def _proj_kernel(x_ref, g_ref, w_ref, c_ref, s1_ref, s2_ref, *out_refs, sample):
    h = _rms(x_ref[...], g_ref[...]).astype(BF16)
    proj = jnp.dot(h, w_ref[...], preferred_element_type=F32)
    c, s1, s2 = c_ref[...], s1_ref[...], s2_ref[...]
    q = _rope(proj[:, :ATTN_WIDTH], c, s1, s2) * (1.0 / math.sqrt(HEAD_DIM))
    k = _rope(proj[:, ATTN_WIDTH:2 * ATTN_WIDTH], c, s1, s2)
    v = proj[:, 2 * ATTN_WIDTH:3 * ATTN_WIDTH]
    if sample:
        q_ref, k_ref, v_ref, vlin_ref, u_ref = out_refs
        q_ref[...] = q
        k_ref[...] = k
        v_ref[...] = v
    else:
        qt_ref, kb_ref, kt_ref, vt_ref, vlin_ref, u_ref = out_refs
        qt_ref[0] = q.T.astype(BF16)
        kb_ref[...] = k.astype(BF16)
        kt_ref[0] = k.T
        vt_ref[0] = v.T.astype(BF16)
    for hh in range(N_HEADS):
        vlin_ref[pl.ds(hh, v.shape[0], stride=N_HEADS), :] = v[:, hh * LANES:(hh + 1) * LANES]
    u_ref[...] = proj[:, 3 * ATTN_WIDTH:]


def _project(x2d, g, w_bf, tabs, n_tab_tiles, rows_per_group, sample):
    m = x2d.shape[0]
    t = TOK_TILE
    groups = m // rows_per_group
    tiles_per_group = rows_per_group // t
    c, s1, s2 = tabs
    tab_spec = pl.BlockSpec((t, LANES), lambda i: (i % n_tab_tiles, 0))
    ft_spec = pl.BlockSpec((1, ATTN_WIDTH, t), lambda i: (i // tiles_per_group, 0, i % tiles_per_group))
    row_spec = pl.BlockSpec((t, ATTN_WIDTH), lambda i: (i, 0))
    lin_spec = pl.BlockSpec((t * N_HEADS, LANES), lambda i: (i, 0))
    row = lambda dt: jax.ShapeDtypeStruct((m, ATTN_WIDTH), dt)
    ft = lambda dt: jax.ShapeDtypeStruct((groups, ATTN_WIDTH, rows_per_group), dt)
    lin = jax.ShapeDtypeStruct((m * N_HEADS, LANES), F32)
    if sample:
        out_specs = [row_spec, row_spec, row_spec, lin_spec, row_spec]
        out_shape = [row(F32), row(F32), row(F32), lin, row(F32)]
    else:
        out_specs = [ft_spec, row_spec, ft_spec, ft_spec, lin_spec, row_spec]
        out_shape = [ft(BF16), row(BF16), ft(F32), ft(BF16), lin, row(F32)]
    return pl.pallas_call(
        functools.partial(_proj_kernel, sample=sample),
        grid=(m // t,),
        in_specs=[pl.BlockSpec((t, D_MODEL), lambda i: (i, 0)),
                  _const_spec((1, D_MODEL)),
                  _const_spec((D_MODEL, 4 * ATTN_WIDTH)),
                  tab_spec, tab_spec, tab_spec],
        out_specs=out_specs,
        out_shape=out_shape,
        compiler_params=_params(("arbitrary",)),
        name="proj",
    )(x2d, g, w_bf, c, s1, s2)


def _rope_tables(pos):
    half = ROT_DIM // 2
    inv_freq = jnp.float32(ROPE_THETA) ** (-jnp.arange(half, dtype=F32) * 2.0 / ROT_DIM)
    ang = pos.astype(F32)[:, None] * inv_freq[None, :]
    cos, sin = jnp.cos(ang), jnp.sin(ang)
    n = pos.shape[0]
    pad = jnp.zeros((n, HEAD_DIM - ROT_DIM), F32)
    c = jnp.concatenate([cos, cos, pad + 1.0], axis=1)
    s1 = jnp.concatenate([-sin, jnp.zeros_like(sin), pad], axis=1)
    s2 = jnp.concatenate([jnp.zeros_like(sin), sin, pad], axis=1)
    tile2 = lambda a: jnp.concatenate([a, a], axis=1)
    return tile2(c), tile2(s1), tile2(s2)


def _lam(lq1, lk1, lq2, lk2):
    return (jnp.exp(jnp.sum(lq1 * lk1, keepdims=True)) - jnp.exp(jnp.sum(lq2 * lk2, keepdims=True))
            + LAM_INIT)


def _subln(o, gain):
    ms = jnp.mean(o * o, axis=-1, keepdims=True)
    return o * lax.rsqrt(ms + EPS) * gain * (1.0 - LAM_INIT)


ATTN_COLS = 256


def _prompt_attn_kernel(qt_ref, k_ref, vt_ref, lq1, lk1, lq2, lk2, gain_ref, o_ref,
                        q2_s, m_s, l_s, acc_s, *, tq):
    qi = pl.program_id(2)
    qt = qt_ref[0]
    feat = lax.broadcasted_iota(jnp.int32, qt.shape, 0)
    zero = jnp.zeros_like(qt)
    q2_s[:, 0:tq] = jnp.where(feat < HEAD_DIM, qt, zero)
    q2_s[:, tq:2 * tq] = jnp.where(feat >= HEAD_DIM, qt, zero)
    m_s[...] = jnp.full(m_s.shape, NEG_INF, F32)
    l_s[...] = jnp.zeros(l_s.shape, F32)
    acc_s[...] = jnp.zeros(acc_s.shape, F32)

    def step(ki, masked):
        start = pl.multiple_of(ki * tq, tq)
        k = k_ref[pl.ds(start, tq), :]
        vt = vt_ref[0, :, pl.ds(start, tq)]
        groups = [slice(g * ATTN_COLS, (g + 1) * ATTN_COLS) for g in range(2 * tq // ATTN_COLS)]
        n_keys = [(cols.stop - 1) % tq + 1 if masked else tq for cols in groups]
        scores = [jnp.dot(k[:nk], q2_s[:, cols], preferred_element_type=F32) for cols, nk in zip(groups, n_keys)]
        probs, stats = [], []
        for cols, s in zip(groups, scores):
            if masked:
                key = lax.broadcasted_iota(jnp.int32, s.shape, 0)
                qry = (lax.broadcasted_iota(jnp.int32, s.shape, 1) + cols.start) % tq
                s = jnp.where(key <= qry, s, NEG_INF)
            m_old = m_s[:, cols]
            m_new = jnp.maximum(m_old, jnp.max(s, axis=0, keepdims=True))
            alpha = jnp.exp(m_old - m_new)
            p = jnp.exp(s - m_new)
            probs.append(p.astype(BF16))
            stats.append((m_new, alpha, alpha * l_s[:, cols] + jnp.sum(p, axis=0, keepdims=True)))
        for cols, nk, p, (m_new, alpha, l_new) in zip(groups, n_keys, probs, stats):
            acc_s[:, cols] = alpha * acc_s[:, cols] + jnp.dot(vt[:, :nk], p, preferred_element_type=F32)
            m_s[:, cols] = m_new
            l_s[:, cols] = l_new

    def body(ki, carry):
        step(ki, False)
        return carry

    lax.fori_loop(0, qi, body, 0)
    step(qi, True)

    lam = _lam(lq1[...], lk1[...], lq2[...], lk2[...])
    o = acc_s[...] / l_s[...]
    o = o[:, 0:tq] - lam * o[:, tq:2 * tq]
    ms = jnp.mean(o * o, axis=0, keepdims=True)
    o = o * lax.rsqrt(ms + EPS) * gain_ref[...] * (1.0 - LAM_INIT)
    o_ref[...] = o.T.astype(o_ref.dtype)


def _prompt_attention(qt_bf, k_bf, vt_bf, lam_params, gain_col, n_batch, seq, tq=512):
    nq = seq // tq
    lam_spec = _const_spec((1, HEAD_DIM))
    return pl.pallas_call(
        functools.partial(_prompt_attn_kernel, tq=tq),
        grid=(n_batch, N_HEADS, nq),
        in_specs=[pl.BlockSpec((1, LANES, tq), lambda n, h, i: (n, h, i)),
                  pl.BlockSpec((seq, LANES), lambda n, h, i: (n, h)),
                  pl.BlockSpec((1, LANES, seq), lambda n, h, i: (n, h, 0)),
                  lam_spec, lam_spec, lam_spec, lam_spec,
                  _const_spec((LANES, 1))],
        out_specs=pl.BlockSpec((tq, LANES), lambda n, h, i: (n * nq + i, h)),
        out_shape=jax.ShapeDtypeStruct((n_batch * seq, ATTN_WIDTH), BF16),
        scratch_shapes=[pltpu.VMEM((LANES, 2 * tq), BF16),
                        pltpu.VMEM((1, 2 * tq), F32),
                        pltpu.VMEM((1, 2 * tq), F32),
                        pltpu.VMEM((LANES, 2 * tq), F32)],
        compiler_params=_params(("arbitrary", "arbitrary", "arbitrary")),
        name="prompt_attn",
    )(qt_bf, k_bf, vt_bf, *lam_params, gain_col)


SEQS_PER_STEP = 2


def _paged_mlp_kernel(pt_ref, q_ref, knew_ref, vnew_ref, lq1, lk1, lq2, lk2, gain_ref,
                      x_ref, g_ref, gf_ref, ck_hbm, cv_hbm, wu_hbm, wd_hbm,
                      o_ref, y_ref, kbuf, vbuf, sem, wu_s, wd_s, wsem, h_s, acc_s,
                      *, n_steps, n_chunks, t_new):
    g_idx = pl.program_id(0)
    ppc = PAGES_PER_CHUNK
    spg = SEQS_PER_STEP
    rows = 2 * N_HEADS * t_new
    n_iter = spg * n_chunks
    ff_chunk = D_FF // n_iter

    def page_copies(seq, chunk, slot):
        cps = []
        for pg in range(ppc):
            page = pt_ref[chunk * ppc + pg, seq]
            cps.append(pltpu.make_async_copy(ck_hbm.at[page], kbuf.at[slot, pg], sem.at[0, slot]))
            cps.append(pltpu.make_async_copy(cv_hbm.at[page], vbuf.at[slot, pg], sem.at[1, slot]))
        return cps

    def weight_copies():
        return [pltpu.make_async_copy(wu_hbm, wu_s, wsem.at[0]), pltpu.make_async_copy(wd_hbm, wd_s, wsem.at[1])]

    @pl.when(g_idx == 0)
    def _():
        for cp in page_copies(0, 0, 0):
            cp.start()
        for cp in weight_copies():
            cp.start()
        for cp in weight_copies():
            cp.wait()

    def mlp_part(j):
        if j == 0:
            x = x_ref[...]
            h_s[...] = _rms(x, g_ref[...]).astype(BF16)
            acc_s[...] = x
        z = jnp.dot(h_s[...], wu_s[:, j * ff_chunk:(j + 1) * ff_chunk], preferred_element_type=F32)
        a = jnp.square(jnp.maximum(z, 0.0)).astype(BF16)
        acc_s[...] += jnp.dot(a, wd_s[j * ff_chunk:(j + 1) * ff_chunk, :], preferred_element_type=F32)
        if j == n_iter - 1:
            y_ref[...] = _rms(acc_s[...], gf_ref[...])

    lam = _lam(lq1[...], lk1[...], lq2[...], lk2[...])
    gain = gain_ref[...]

    for j in range(n_iter):
        i, c = divmod(j, n_chunks)
        seq = g_idx * spg + i
        slot = j % 2
        nslot = (j + 1) % 2
        if j + 1 < n_iter:
            i2, c2 = divmod(j + 1, n_chunks)
            for cp in page_copies(g_idx * spg + i2, c2, nslot):
                cp.start()
        else:
            @pl.when(g_idx + 1 < n_steps)
            def _():
                for cp in page_copies((g_idx + 1) * spg, 0, nslot):
                    cp.start()
        mlp_part(j)
        for cp in page_copies(seq, c, slot):
            cp.wait()
        if c == 0:
            q = q_ref[i * t_new:(i + 1) * t_new, :]
            qt = jnp.concatenate([q] * (2 * N_HEADS), axis=0)
            r_hj = lax.broadcasted_iota(jnp.int32, qt.shape, 0) // t_new
            c_hj = lax.broadcasted_iota(jnp.int32, qt.shape, 1) // HEAD_DIM
            qbd = jnp.where(r_hj == c_hj, qt, 0.0).astype(BF16)
            m = jnp.full((rows, 1), NEG_INF, F32)
            l = jnp.zeros((rows, 1), F32)
            acc = jnp.zeros((rows, ATTN_WIDTH), F32)
        kt = jnp.concatenate([kbuf[slot, pg].astype(BF16) for pg in range(ppc)], axis=1)
        s = jnp.dot(qbd, kt, preferred_element_type=F32)
        m_new = jnp.maximum(m, jnp.max(s, axis=-1, keepdims=True))
        alpha = jnp.exp(m - m_new)
        p = jnp.exp(s - m_new)
        l = alpha * l + jnp.sum(p, axis=-1, keepdims=True)
        v = jnp.concatenate(
            [jnp.concatenate([vbuf[slot, pg, pl.ds(hh, PAGE_SIZE, stride=N_HEADS), :] for hh in range(N_HEADS)],
                             axis=1) for pg in range(ppc)], axis=0).astype(BF16)
        acc = alpha * acc + jnp.dot(p.astype(BF16), v, preferred_element_type=F32)
        m = m_new
        if c < n_chunks - 1:
            continue

        pad = jnp.zeros((t_new, ATTN_WIDTH), F32)
        knew = jnp.concatenate([knew_ref[i * t_new:(i + 1) * t_new, :], pad], axis=0).astype(BF16)
        vnew = jnp.concatenate([vnew_ref[i * t_new:(i + 1) * t_new, :], pad], axis=0).astype(BF16)
        s = lax.dot_general(qbd, knew, (((1,), (1,)), ((), ())), preferred_element_type=F32)
        row_t = lax.broadcasted_iota(jnp.int32, s.shape, 0) % t_new
        col_t = lax.broadcasted_iota(jnp.int32, s.shape, 1)
        s = jnp.where(col_t <= row_t, s, NEG_INF)
        m_new = jnp.maximum(m, jnp.max(s, axis=-1, keepdims=True))
        alpha = jnp.exp(m - m_new)
        p = jnp.exp(s - m_new)
        l = alpha * l + jnp.sum(p, axis=-1, keepdims=True)
        acc = alpha * acc + jnp.dot(p.astype(BF16), vnew, preferred_element_type=F32)
        o_all = acc / l
        outs = []
        for hh in range(N_HEADS):
            r0 = hh * 2 * t_new
            blk = o_all[r0:r0 + 2 * t_new, hh * LANES:(hh + 1) * LANES]
            o = blk[0:t_new, :] - lam * blk[t_new:2 * t_new, :]
            outs.append(_subln(o, gain))
        o_ref[i * t_new:(i + 1) * t_new, :] = jnp.concatenate(outs, axis=1)


def _paged_mlp(pt_t, q, knew, vnew, lam_params, gain, ck, cv, x2d, g, gf, wu, wd, n_seq, t_new):
    n_pages = pt_t.shape[0]
    n_chunks = n_pages // PAGES_PER_CHUNK
    n_steps = n_seq // SEQS_PER_STEP
    m = x2d.shape[0]
    t = m // n_steps
    assert n_seq % SEQS_PER_STEP == 0 and m % n_steps == 0 and t % SUBLANES == 0
    assert D_FF % (SEQS_PER_STEP * n_chunks) == 0 and (SEQS_PER_STEP * n_chunks) % 2 == 0
    cmap = lambda s, pt: (0, 0)
    lam_spec = pl.BlockSpec((1, HEAD_DIM), cmap)
    row_spec = pl.BlockSpec((SEQS_PER_STEP * t_new, ATTN_WIDTH), lambda s, pt: (s, 0))
    x_spec = pl.BlockSpec((t, D_MODEL), lambda s, pt: (s, 0))
    vec_spec = pl.BlockSpec((1, D_MODEL), cmap)
    any_spec = pl.BlockSpec(memory_space=pl.ANY)
    feat = ck.shape[1]
    grid_spec = pltpu.PrefetchScalarGridSpec(
        num_scalar_prefetch=1,
        grid=(n_steps,),
        in_specs=[row_spec, row_spec, row_spec,
                  lam_spec, lam_spec, lam_spec, lam_spec,
                  pl.BlockSpec((1, LANES), cmap),
                  x_spec, vec_spec, vec_spec,
                  any_spec, any_spec, any_spec, any_spec],
        out_specs=[row_spec, x_spec],
        scratch_shapes=[pltpu.VMEM((2, PAGES_PER_CHUNK, feat, PAGE_SIZE), F32),
                        pltpu.VMEM((2, PAGES_PER_CHUNK, PAGE_SIZE * N_HEADS, LANES), F32),
                        pltpu.SemaphoreType.DMA((2, 2)),
                        pltpu.VMEM((D_MODEL, D_FF), BF16),
                        pltpu.VMEM((D_FF, D_MODEL), BF16),
                        pltpu.SemaphoreType.DMA((2,)),
                        pltpu.VMEM((t, D_MODEL), BF16),
                        pltpu.VMEM((t, D_MODEL), F32)],
    )
    return pl.pallas_call(
        functools.partial(_paged_mlp_kernel, n_steps=n_steps, n_chunks=n_chunks, t_new=t_new),
        grid_spec=grid_spec,
        out_shape=[jax.ShapeDtypeStruct((n_seq * t_new, ATTN_WIDTH), F32),
                   jax.ShapeDtypeStruct((m, D_MODEL), F32)],
        compiler_params=_params(("arbitrary",)),
        name="paged_mlp",
    )(pt_t, q, knew, vnew, *lam_params, gain, x2d, g, gf, ck, cv, wu, wd)


HALF_STATE = N_STATE // 2
S5_TIME_TILE = 32


def _s5_tail(x_bf, u, cc_ref, d_ref, wg_ref, bg_ref):
    y = jnp.concatenate(
        [jnp.dot(x_bf[:, hf * N_STATE:(hf + 1) * N_STATE], cc_ref[hf], preferred_element_type=F32) for hf in range(2)],
        axis=1) + d_ref[...] * u
    g = 0.5 * y * (1.0 + lax.erf(y * (1.0 / math.sqrt(2.0))))
    z = jnp.dot(g.astype(BF16), wg_ref[...], preferred_element_type=F32) + bg_ref[...]
    return g * (1.0 / (1.0 + jnp.exp(-z)))


def _s5_bu(u_bf, bb_ref, bu_s):
    half_u = SSM_WIDTH // 2
    for hf in range(2):
        bu_s[:, hf * N_STATE:(hf + 1) * N_STATE] = jnp.dot(u_bf[:, hf * half_u:(hf + 1) * half_u], bb_ref[hf],
                                                           preferred_element_type=F32)


def _s5_prompt_kernel(u_ref, perm_ref, permt_ref, bb_ref, cc_ref, lb_ref, d_ref, wg_ref, bg_ref,
                      ys_ref, ht_ref, bu_s, carry_s):
    nb, tt = u_ref.shape[0], u_ref.shape[1]
    rows = nb * tt

    @pl.when(pl.program_id(0) == 0)
    def _():
        carry_s[...] = jnp.zeros(carry_s.shape, F32)

    u = u_ref[...].reshape(rows, SSM_WIDTH)
    hi = u.astype(BF16)
    r1 = u - hi.astype(F32)
    mid = r1.astype(BF16)
    lo = (r1 - mid.astype(F32)).astype(BF16)
    perm = perm_ref[...]
    u_hi = jnp.dot(perm, hi, preferred_element_type=F32)
    u_tb = u_hi + jnp.dot(perm, mid, preferred_element_type=F32) + jnp.dot(perm, lo, preferred_element_type=F32)
    _s5_bu(u_hi.astype(BF16), bb_ref, bu_s)

    for hf in range(2):
        c0 = hf * N_STATE
        lbr = lb_ref[0, :, hf * HALF_STATE:(hf + 1) * HALF_STATE]
        lbi = lb_ref[1, :, hf * HALF_STATE:(hf + 1) * HALF_STATE]

        def body(t, carry, c0=c0, lbr=lbr, lbi=lbi):
            xr, xi = carry
            r0 = pl.multiple_of(t * nb, nb)
            br = bu_s[pl.ds(r0, nb), c0:c0 + HALF_STATE]
            bi = bu_s[pl.ds(r0, nb), c0 + HALF_STATE:c0 + N_STATE]
            xr, xi = lbr * xr - lbi * xi + br, lbr * xi + lbi * xr + bi
            bu_s[pl.ds(r0, nb), c0:c0 + HALF_STATE] = xr
            bu_s[pl.ds(r0, nb), c0 + HALF_STATE:c0 + N_STATE] = xi
            return xr, xi

        xr, xi = lax.fori_loop(0, tt, body, (carry_s[:, c0:c0 + HALF_STATE], carry_s[:, c0 + HALF_STATE:c0 + N_STATE]),
                               unroll=4)
        carry_s[:, c0:c0 + HALF_STATE] = xr
        carry_s[:, c0 + HALF_STATE:c0 + N_STATE] = xi
    ht_ref[...] = carry_s[...]

    ys_tb = _s5_tail(bu_s[...].astype(BF16), u_tb, cc_ref, d_ref, wg_ref, bg_ref).astype(BF16)
    ys = jnp.dot(permt_ref[...], ys_tb, preferred_element_type=F32)
    ys_ref[...] = ys.astype(ys_ref.dtype).reshape(nb, tt, SSM_WIDTH)


def _s5_prompt(u3, ssm):
    bb, cc, lb_tab, a_tab, pw_tab, d_row, wg, bg = ssm
    nb, seq = u3.shape[0], u3.shape[1]
    tt = S5_TIME_TILE
    rows = nb * tt
    r = jnp.arange(rows)
    perm = jax.nn.one_hot((r % nb) * tt + r // nb, rows, dtype=BF16)
    blk = pl.BlockSpec((nb, tt, SSM_WIDTH), lambda i: (0, i, 0))
    return pl.pallas_call(
        _s5_prompt_kernel,
        grid=(seq // tt,),
        in_specs=[blk, _const_spec((rows, rows)), _const_spec((rows, rows)),
                  _const_spec(bb.shape), _const_spec(cc.shape), _const_spec(lb_tab.shape),
                  _const_spec(d_row.shape), _const_spec(wg.shape), _const_spec(bg.shape)],
        out_specs=[blk, _const_spec((nb, 2 * N_STATE))],
        out_shape=[jax.ShapeDtypeStruct((nb, seq, SSM_WIDTH), BF16),
                   jax.ShapeDtypeStruct((nb, 2 * N_STATE), F32)],
        scratch_shapes=[pltpu.VMEM((rows, 2 * N_STATE), F32), pltpu.VMEM((nb, 2 * N_STATE), F32)],
        compiler_params=_params(("arbitrary",)),
        name="s5_prompt",
    )(u3, perm, perm.T, bb, cc, lb_tab, d_row, wg, bg)


def _s5_sample_kernel(u_ref, h0_ref, bb_ref, cc_ref, a_ref, pw_ref, d_ref, wg_ref, bg_ref,
                      ys_ref, ht_ref, bu_s):
    t = u_ref.shape[0]
    u = u_ref[...]
    _s5_bu(u.astype(BF16), bb_ref, bu_s)

    def cmul_add(xr, xi, ar, ai, sr, si):
        return xr + ar * sr - ai * si, xi + ar * si + ai * sr

    def body(b, carry):
        r0 = pl.multiple_of(b * SUBLANES, SUBLANES)
        init = h0_ref[pl.ds(b, 1), :]
        for hf in range(2):
            c0 = hf * N_STATE
            st = slice(hf * HALF_STATE, (hf + 1) * HALF_STATE)
            xr = bu_s[pl.ds(r0, SUBLANES), c0:c0 + HALF_STATE]
            xi = bu_s[pl.ds(r0, SUBLANES), c0 + HALF_STATE:c0 + N_STATE]
            for k, shift in enumerate((1, 2, 4)):
                sr = pltpu.roll(xr, shift, 0)
                si = pltpu.roll(xi, shift, 0)
                xr, xi = cmul_add(xr, xi, a_ref[k, 0, :, st], a_ref[k, 1, :, st], sr, si)
            cr = jnp.broadcast_to(init[:, c0:c0 + HALF_STATE], xr.shape)
            ci = jnp.broadcast_to(init[:, c0 + HALF_STATE:c0 + N_STATE], xi.shape)
            xr, xi = cmul_add(xr, xi, pw_ref[0, :, st], pw_ref[1, :, st], cr, ci)
            bu_s[pl.ds(r0, SUBLANES), c0:c0 + HALF_STATE] = xr
            bu_s[pl.ds(r0, SUBLANES), c0 + HALF_STATE:c0 + N_STATE] = xi
            ht_ref[pl.ds(b, 1), c0:c0 + HALF_STATE] = xr[SUBLANES - 1:SUBLANES, :]
            ht_ref[pl.ds(b, 1), c0 + HALF_STATE:c0 + N_STATE] = xi[SUBLANES - 1:SUBLANES, :]
        return carry

    lax.fori_loop(0, t // SUBLANES, body, 0)
    ys_ref[...] = _s5_tail(bu_s[...].astype(BF16), u, cc_ref, d_ref, wg_ref, bg_ref).astype(ys_ref.dtype)


def _s5_sample(u, h0, ssm):
    bb, cc, lb_tab, a_tab, pw_tab, d_row, wg, bg = ssm
    t = TOK_TILE
    m = u.shape[0]
    h_spec = pl.BlockSpec((t // SUBLANES, 2 * N_STATE), lambda i: (i, 0))
    return pl.pallas_call(
        _s5_sample_kernel,
        grid=(m // t,),
        in_specs=[pl.BlockSpec((t, SSM_WIDTH), lambda i: (i, 0)), h_spec,
                  _const_spec(bb.shape), _const_spec(cc.shape), _const_spec(a_tab.shape), _const_spec(pw_tab.shape),
                  _const_spec(d_row.shape), _const_spec(wg.shape), _const_spec(bg.shape)],
        out_specs=[pl.BlockSpec((t, SSM_WIDTH), lambda i: (i, 0)), h_spec],
        out_shape=[jax.ShapeDtypeStruct((m, SSM_WIDTH), BF16),
                   jax.ShapeDtypeStruct((m // SUBLANES, 2 * N_STATE), F32)],
        scratch_shapes=[pltpu.VMEM((t, 2 * N_STATE), F32)],
        compiler_params=_params(("arbitrary",)),
        name="s5_sample",
    )(u, h0, bb, cc, a_tab, pw_tab, d_row, wg, bg)


def _state_pack(re, im):
    return jnp.concatenate([re[:, :HALF_STATE], im[:, :HALF_STATE], re[:, HALF_STATE:], im[:, HALF_STATE:]], axis=1)


def _state_unpack(x):
    re = jnp.concatenate([x[:, 0:HALF_STATE], x[:, N_STATE:N_STATE + HALF_STATE]], axis=1)
    im = jnp.concatenate([x[:, HALF_STATE:N_STATE], x[:, N_STATE + HALF_STATE:]], axis=1)
    return re, im


def _ssm_tables(a_re, a_im, log_dt, b_re, b_im, c_re, c_im, d_skip, w_glu, b_glu):
    delta = jnp.exp(log_dt)[:, None]
    mag = jnp.exp(a_re * delta)
    ang = a_im * delta
    lb_re = mag * jnp.cos(ang)
    lb_im = mag * jnp.sin(ang)
    den = a_re * a_re + a_im * a_im
    num_re = lb_re - 1.0
    cz_re = (num_re * a_re + lb_im * a_im) / den
    cz_im = (lb_im * a_re - num_re * a_im) / den
    bb_re = cz_re[..., None] * b_re - cz_im[..., None] * b_im
    bb_im = cz_re[..., None] * b_im + cz_im[..., None] * b_re
    gh = SSM_GROUPS // 2
    eye = jnp.eye(gh, dtype=F32)
    bd_in = lambda w: jnp.einsum('gsp,gh->gphs', w, eye).reshape(gh * SSM_GROUP, HALF_STATE)
    bd_out = lambda w: jnp.einsum('gps,gh->gshp', w, eye).reshape(HALF_STATE, gh * SSM_GROUP)
    halves = lambda w: (w[:gh], w[gh:])
    bb = jnp.stack([jnp.concatenate([bd_in(r), bd_in(i)], axis=1)
                    for r, i in zip(halves(bb_re), halves(bb_im))]).astype(BF16)
    cc = jnp.stack([jnp.concatenate([bd_out(r), bd_out(-i)], axis=0)
                    for r, i in zip(halves(c_re), halves(c_im))]).astype(BF16)

    def cmul(ar, ai, br, bi):
        return ar * br - ai * bi, ar * bi + ai * br

    l1 = (lb_re.reshape(N_STATE), lb_im.reshape(N_STATE))
    pows = [l1]
    for _ in range(SUBLANES - 1):
        pows.append(cmul(*pows[-1], *l1))
    rows = jnp.arange(SUBLANES)[:, None]
    a_tab = jnp.stack([jnp.stack([jnp.where(rows >= sh, pows[sh - 1][0][None, :], 0.0),
                                  jnp.where(rows >= sh, pows[sh - 1][1][None, :], 0.0)]) for sh in (1, 2, 4)])
    pw_tab = jnp.stack([jnp.stack([p[0] for p in pows]), jnp.stack([p[1] for p in pows])])
    lb_tab = jnp.stack([jnp.broadcast_to(l1[0][None, :], (SUBLANES, N_STATE)),
                        jnp.broadcast_to(l1[1][None, :], (SUBLANES, N_STATE))])
    return (bb, cc, lb_tab, a_tab.astype(F32), pw_tab, d_skip.reshape(1, SSM_WIDTH),
            w_glu.astype(BF16), b_glu.reshape(1, SSM_WIDTH))


def _memkv_kernel(mem_ref, g_ref, wk_ref, wv_ref, kt_ref, vt_ref, ktb_ref, vtb_ref):
    mn = _rms(mem_ref[...], g_ref[...]).astype(BF16)
    kt = jnp.dot(mn, wk_ref[...], preferred_element_type=F32).T
    vt = jnp.dot(mn, wv_ref[...], preferred_element_type=F32).T
    kt_ref[0] = kt
    vt_ref[0] = vt
    ktb_ref[0] = kt.astype(BF16)
    vtb_ref[0] = vt.astype(BF16)


def _memory_kv(mem2d, g, wk, wv, n_batch):
    spec = pl.BlockSpec((1, CROSS_WIDTH, N_MEM), lambda n: (n, 0, 0))
    f = jax.ShapeDtypeStruct((n_batch, CROSS_WIDTH, N_MEM), F32)
    b = jax.ShapeDtypeStruct((n_batch, CROSS_WIDTH, N_MEM), BF16)
    return pl.pallas_call(
        _memkv_kernel,
        grid=(n_batch,),
        in_specs=[pl.BlockSpec((N_MEM, D_MODEL), lambda n: (n, 0)), _const_spec((1, D_MODEL)),
                  _const_spec((D_MODEL, CROSS_WIDTH)), _const_spec((D_MODEL, CROSS_WIDTH))],
        out_specs=[spec, spec, spec, spec],
        out_shape=[f, f, b, b],
        compiler_params=_params(("arbitrary",)),
        name="memory_kv",
    )(mem2d, g, wk, wv)


def _cross_attend(hq, kt, vt, n_q):
    heads = CROSS_WIDTH // CROSS_HEAD_DIM
    stacked = jnp.concatenate([hq] * heads, axis=0)
    r_h = lax.broadcasted_iota(jnp.int32, stacked.shape, 0) // n_q
    c_h = lax.broadcasted_iota(jnp.int32, stacked.shape, 1) // CROSS_HEAD_DIM
    own = r_h == c_h
    s = jnp.dot(jnp.where(own, stacked, 0.0).astype(BF16), kt, preferred_element_type=F32)
    p = jnp.exp(s - jnp.max(s, axis=-1, keepdims=True))
    p = p / jnp.sum(p, axis=-1, keepdims=True)
    full = lax.dot_general(p.astype(BF16), vt, (((1,), (1,)), ((), ())), preferred_element_type=F32)
    full = jnp.where(own, full, 0.0)
    out = full[0:n_q, :]
    for hh in range(1, heads):
        out = out + full[hh * n_q:(hh + 1) * n_q, :]
    return out


def _mix_cross_kernel(x_ref, o_ref, ys_ref, wo_ref, g_ref, wq_ref, kt_ref, vt_ref, wco_ref, out_ref, *, seq_rows):
    o = o_ref[...].astype(BF16)
    mix = (jnp.dot(o, wo_ref[0:ATTN_WIDTH, :], preferred_element_type=F32)
           + jnp.dot(ys_ref[...], wo_ref[ATTN_WIDTH:, :], preferred_element_type=F32))
    x1 = x_ref[...] + mix
    hq = jnp.dot(_rms(x1, g_ref[...]).astype(BF16), wq_ref[...], preferred_element_type=F32)
    hq = hq * (1.0 / math.sqrt(CROSS_HEAD_DIM))
    pieces = []
    for j in range(x1.shape[0] // seq_rows):
        pieces.append(_cross_attend(hq[j * seq_rows:(j + 1) * seq_rows, :], kt_ref[j].astype(BF16),
                                    vt_ref[j].astype(BF16), seq_rows))
    oc = jnp.concatenate(pieces, axis=0)
    out_ref[...] = x1 + jnp.dot(oc.astype(BF16), wco_ref[...], preferred_element_type=F32)


def _mix_cross(x2d, o, ys, wo, g, wq, kt, vt, wco, seq_rows, t, tiles_per_mem):
    m = x2d.shape[0]
    mem_spec = pl.BlockSpec((t // seq_rows, CROSS_WIDTH, N_MEM), lambda i: (i // tiles_per_mem, 0, 0))
    return pl.pallas_call(
        functools.partial(_mix_cross_kernel, seq_rows=seq_rows),
        grid=(m // t,),
        in_specs=[pl.BlockSpec((t, D_MODEL), lambda i: (i, 0)),
                  pl.BlockSpec((t, ATTN_WIDTH), lambda i: (i, 0)),
                  pl.BlockSpec((t, SSM_WIDTH), lambda i: (i, 0)),
                  _const_spec((D_MODEL, D_MODEL)), _const_spec((1, D_MODEL)),
                  _const_spec((D_MODEL, CROSS_WIDTH)), mem_spec, mem_spec,
                  _const_spec((CROSS_WIDTH, D_MODEL))],
        out_specs=pl.BlockSpec((t, D_MODEL), lambda i: (i, 0)),
        out_shape=jax.ShapeDtypeStruct((m, D_MODEL), F32),
        compiler_params=_params(("arbitrary",)),
        name="mix_cross",
    )(x2d, o, ys, wo, g, wq, kt, vt, wco)


def _mlp_kernel(x_ref, g_ref, wu_ref, wd_ref, gf_ref, y_ref, *, ff_chunk):
    x = x_ref[...]
    h = _rms(x, g_ref[...]).astype(BF16)
    acc = x
    for c in range(D_FF // ff_chunk):
        z = jnp.dot(h, wu_ref[:, c * ff_chunk:(c + 1) * ff_chunk], preferred_element_type=F32)
        a = jnp.square(jnp.maximum(z, 0.0)).astype(BF16)
        acc = acc + jnp.dot(a, wd_ref[c * ff_chunk:(c + 1) * ff_chunk, :], preferred_element_type=F32)
    y_ref[...] = _rms(acc, gf_ref[...])


def _mlp(x2d, g, wu, wd, gf, ff_chunk=1024):
    m = x2d.shape[0]
    t = TOK_TILE
    return pl.pallas_call(
        functools.partial(_mlp_kernel, ff_chunk=ff_chunk),
        grid=(m // t,),
        in_specs=[pl.BlockSpec((t, D_MODEL), lambda i: (i, 0)), _const_spec((1, D_MODEL)),
                  _const_spec((D_MODEL, D_FF)), _const_spec((D_FF, D_MODEL)), _const_spec((1, D_MODEL))],
        out_specs=pl.BlockSpec((t, D_MODEL), lambda i: (i, 0)),
        out_shape=jax.ShapeDtypeStruct((m, D_MODEL), F32),
        compiler_params=_params(("arbitrary",)),
        name="mlp",
    )(x2d, g, wu, wd, gf)


def kernel(x_prompt, x_sample, mem_prompt, cache_k, cache_v, page_table, state_ssm_re, state_ssm_im, cache_mem_k, cache_mem_v, norm_mix, w_in, lambda_q1, lambda_k1, lambda_q2, lambda_k2, subln_gain, ssm_a_re, ssm_a_im, ssm_log_dt, ssm_b_re, ssm_b_im, ssm_c_re, ssm_c_im, ssm_d, w_glu, b_glu, w_out, norm_cross, norm_mem, w_cq, w_ck, w_cv, w_co, norm_mlp, w_up, w_down, final_norm):
    n_p, t_p = x_prompt.shape[0], x_prompt.shape[1]
    n_s, t_s = x_sample.shape[0], x_sample.shape[1]
    n_pool = cache_k.shape[1]
    past = page_table.shape[1] * PAGE_SIZE
    assert cache_k.shape[0] == 1 and t_s == SUBLANES and n_p == SUBLANES
    assert t_p % TOK_TILE == 0 and (n_s * t_s) % TOK_TILE == 0

    l = 0
    w_in_b = w_in[l].astype(BF16)
    w_out_b = w_out[l].astype(BF16)
    w_cq_b, w_ck_b, w_cv_b, w_co_b = (w[l].astype(BF16) for w in (w_cq, w_ck, w_cv, w_co))
    w_up_b, w_down_b = w_up[l].astype(BF16), w_down[l].astype(BF16)
    lam_params = (lambda_q1, lambda_k1, lambda_q2, lambda_k2)
    ssm = _ssm_tables(ssm_a_re[l], ssm_a_im[l], ssm_log_dt[l], ssm_b_re[l], ssm_b_im[l],
                      ssm_c_re[l], ssm_c_im[l], ssm_d[l], w_glu[l], b_glu[l])
    final_g = final_norm.reshape(1, D_MODEL)

    xp = x_prompt.reshape(n_p * t_p, D_MODEL)
    tabs_p = _rope_tables(jnp.arange(t_p, dtype=jnp.int32))
    qt_p, kb_p, kt_p, vt_p, vlin_p, u_p = _project(xp, norm_mix, w_in_b, tabs_p, t_p // TOK_TILE, t_p, False)
    o_p = _prompt_attention(qt_p, kb_p, vt_p, lam_params, subln_gain.reshape(LANES, 1), n_p, t_p)
    ys_p, ht_p = _s5_prompt(u_p.reshape(n_p, t_p, SSM_WIDTH), ssm)
    mkt, mvt, mktb, mvtb = _memory_kv(mem_prompt.reshape(n_p * N_MEM, D_MODEL), norm_mem, w_ck_b, w_cv_b, n_p)
    x2_p = _mix_cross(xp, o_p, ys_p.reshape(n_p * t_p, SSM_WIDTH), w_out_b, norm_cross, w_cq_b, mktb, mvtb, w_co_b,
                      TOK_TILE, TOK_TILE, t_p // TOK_TILE)

    xs = x_sample.reshape(n_s * t_s, D_MODEL)
    pos_s = past + (jnp.arange(TOK_TILE, dtype=jnp.int32) % t_s)
    tabs_s = _rope_tables(pos_s)
    q_s, k_s, v_s, vlin_s, u_s = _project(xs, norm_mix, w_in_b, tabs_s, 1, n_s * t_s, True)
    ck = jnp.transpose(cache_k, (0, 1, 3, 4, 5, 2)).reshape(n_pool, ATTN_WIDTH, PAGE_SIZE)
    cv = cache_v.reshape(n_pool, PAGE_SIZE * N_HEADS, LANES)
    o_s, y_p = _paged_mlp(page_table.T, q_s, k_s, v_s, lam_params, subln_gain, ck, cv,
                          x2_p, norm_mlp, final_g, w_up_b, w_down_b, n_s, t_s)
    h0_s = _state_pack(state_ssm_re[l].reshape(n_s, N_STATE), state_ssm_im[l].reshape(n_s, N_STATE))
    ys_s, ht_s = _s5_sample(u_s, h0_s, ssm)
    cmk = jnp.transpose(cache_mem_k[l], (0, 2, 3, 1)).reshape(n_s, CROSS_WIDTH, N_MEM)
    cmv = jnp.transpose(cache_mem_v[l], (0, 2, 3, 1)).reshape(n_s, CROSS_WIDTH, N_MEM)
    x2_s = _mix_cross(xs, o_s, ys_s, w_out_b, norm_cross, w_cq_b, cmk, cmv, w_co_b, t_s, 128, 1)
    y_s = _mlp(x2_s, norm_mlp, w_up_b, w_down_b, final_g)

    y_prompt = y_p.reshape(n_p, t_p, D_MODEL)
    y_sample = y_s.reshape(n_s, t_s, D_MODEL)
    k_prompt = jnp.transpose(kt_p.reshape(1, n_p, N_HEADS, 2, HEAD_DIM, t_p), (0, 1, 5, 2, 3, 4))
    v_prompt = vlin_p.reshape(1, n_p, t_p, N_HEADS, 2 * HEAD_DIM)
    state4 = lambda a, n: a.reshape(1, n, SSM_GROUPS, SSM_STATE)
    re_p, im_p = _state_unpack(ht_p)
    re_s, im_s = _state_unpack(ht_s)
    unpack_mem = lambda a: jnp.transpose(a.reshape(1, n_p, CROSS_WIDTH // CROSS_HEAD_DIM, CROSS_HEAD_DIM, N_MEM),
                                         (0, 1, 4, 2, 3))
    k_sample = k_s.reshape(1, n_s, t_s, N_HEADS, 2, HEAD_DIM)
    v_sample = vlin_s.reshape(1, n_s, t_s, N_HEADS, 2 * HEAD_DIM)
    return (y_prompt, y_sample, k_prompt, v_prompt, state4(re_p, n_p), state4(im_p, n_p),
            unpack_mem(mkt), unpack_mem(mvt), k_sample, v_sample, state4(re_s, n_s), state4(im_s, n_s))
```
